```python
import math, functools
import jax, jax.numpy as jnp
from jax import lax
import numpy as np

D_MODEL = 1024
BATCH = 16
SEQ = 4096
DEPTH = 4

GRID_W = 64
CTX_LEN = 256
NORM_EPS = 1e-6

ML_WIDTH = D_MODEL
ML_HEADS = 4
ML_HD = ML_WIDTH // ML_HEADS
ML_CHUNK = 64
ML_M_INIT = -1e30

LRU_WIDTH = D_MODEL
LRU_BLOCKS = 16
LRU_BD = LRU_WIDTH // LRU_BLOCKS
LRU_CONV = 4
LRU_C = 8.0

RW_WIDTH = D_MODEL
RW_HD = 64
RW_HEADS = RW_WIDTH // RW_HD
RW_DECAY_RANK = 64
RW_ICLR_RANK = 64
RW_GATE_RANK = 128
RW_DECAY_SCALE = math.exp(-0.5)
RW_LN_EPS = 64e-5

FFN_HIDDEN = -(-8 * D_MODEL // (3 * 256)) * 256

N_BRANCHES = 3
RW_SPLITS = (RW_WIDTH, RW_WIDTH, RW_WIDTH, 2 * RW_DECAY_RANK, 2 * RW_ICLR_RANK, RW_GATE_RANK)
RW_SEG = 3 * RW_WIDTH + 2 * RW_DECAY_RANK + 2 * RW_ICLR_RANK + RW_GATE_RANK
IN_NAMES = ('ml_q', 'ml_k', 'ml_v', 'ml_o', 'ml_i', 'ml_f', 'lru_x', 'lru_gate', 'rw', 'merge')
IN_SPLITS = (ML_WIDTH, ML_WIDTH, ML_WIDTH, ML_WIDTH, 2 * ML_HEADS, 2 * ML_HEADS, LRU_WIDTH, LRU_WIDTH, RW_SEG, N_BRANCHES * D_MODEL)
IN_WIDTH = sum(IN_SPLITS)

kernel_name = 'hybrid_mlstm_rglru_rwkv7_dit_block'


def split_cols(a, sizes):
    return jnp.split(a, np.cumsum(sizes)[:-1].tolist(), axis=-1)


def rms_norm(x, w):
    xf = x.astype(jnp.float32)
    y = xf * lax.rsqrt(jnp.mean(xf * xf, axis=-1, keepdims=True) + NORM_EPS)
    return (y * w).astype(x.dtype)


def adaln(cond, w, b):
    mod = jax.nn.silu(cond) @ w + b
    return [m[:, None, :] for m in jnp.split(mod, 6, axis=-1)]


def modulate(h, shift, scale):
    return h * (1 + scale) + shift


def grid_transpose(h, rows, cols):
    b, t, d = h.shape
    return h.reshape(b, rows, cols, d).transpose(0, 2, 1, 3).reshape(b, t, d)


def flip_seq(xs):
    return tuple(jnp.flip(a, axis=1) for a in xs)


def bidirectional(scan_f, scan_b, ins_c_f, ins_l_f, ins_c_b, ins_l_b, state0):
    y_c_f, s_f = scan_f(ins_c_f, state0)
    y_l_f, _ = scan_f(ins_l_f, s_f)
    y_c_b, s_b = scan_b(flip_seq(ins_c_b), state0)
    y_l_b, _ = scan_b(flip_seq(ins_l_b), s_b)
    return y_c_f + jnp.flip(y_c_b, axis=1), y_l_f + jnp.flip(y_l_b, axis=1)


def token_shift(x, mu):
    xp = jnp.pad(x, ((0, 0), (1, 1), (0, 0)))
    neighbours = 0.5 * (xp[:, :-2] + xp[:, 2:])
    return x + mu * (neighbours - x)


def dwconv(x, w, b):
    k, ch = w.shape
    left = k // 2
    y = lax.conv_general_dilated(x, w[:, None, :].astype(x.dtype), window_strides=(1,), padding=[(left, k - 1 - left)], dimension_numbers=('NWC', 'WIO', 'NWC'), feature_group_count=ch)
    return y + b


def mlstm_scan(inputs, state0):
    q, k, v, ig, fg = inputs
    b, t, h, dh = q.shape
    nc = t // ML_CHUNK

    def chunks(a):
        a = jnp.swapaxes(a, 1, 2)
        a = a.reshape(b, h, nc, ML_CHUNK, *a.shape[3:])
        return jnp.moveaxis(a, 2, 0)

    mask = jnp.tril(jnp.ones((ML_CHUNK, ML_CHUNK), dtype=bool))

    def step(carry, xs):
        c_st, n_st, m_st = carry
        qc, kc, vc, ic, lfc = xs
        f_cum = jnp.cumsum(lfc, axis=-1)
        logw = f_cum[..., :, None] - f_cum[..., None, :] + ic[..., None, :]
        logw = jnp.where(mask, logw, -jnp.inf)
        inter = f_cum + m_st[..., None]
        m_t = jnp.maximum(jnp.max(logw, axis=-1), inter)
        w_intra = jnp.exp(logw - m_t[..., None])
        w_inter = jnp.exp(inter - m_t)
        s = jnp.einsum('bhtd,bhsd->bhts', qc, kc) * w_intra
        num = jnp.einsum('bhts,bhsd->bhtd', s, vc) + w_inter[..., None] * jnp.einsum('bhvk,bhtk->bhtv', c_st, qc)
        den = jnp.sum(s, axis=-1) + w_inter * jnp.einsum('bhk,bhtk->bht', n_st, qc)
        h_out = num / jnp.maximum(jnp.abs(den), jnp.exp(-m_t))[..., None]
        f_end = f_cum[..., -1]
        log_end = f_end[..., None] - f_cum + ic
        m_new = jnp.maximum(f_end + m_st, jnp.max(log_end, axis=-1))
        w_end = jnp.exp(log_end - m_new[..., None])
        decay = jnp.exp(f_end + m_st - m_new)
        c_new = decay[..., None, None] * c_st + jnp.einsum('bhs,bhsv,bhsk->bhvk', w_end, vc, kc)
        n_new = decay[..., None] * n_st + jnp.einsum('bhs,bhsk->bhk', w_end, kc)
        return (c_new, n_new, m_new), h_out

    xs = (chunks(q), chunks(k), chunks(v), chunks(ig), chunks(jax.nn.log_sigmoid(fg)))
    state, hs = lax.scan(step, state0, xs)
    hs = jnp.moveaxis(hs, 0, 2).reshape(b, h, t, dh)
    return jnp.swapaxes(hs, 1, 2), state


def mlstm_branch(pc, pl, p):
    def prep(pieces):
        b, t = pieces['ml_q'].shape[:2]
        heads = lambda a: a.astype(jnp.float32).reshape(b, t, ML_HEADS, ML_HD)
        q = heads(pieces['ml_q'])
        k = heads(pieces['ml_k']) * ML_HD ** -0.5
        v = heads(pieces['ml_v'])
        ig = pieces['ml_i'].astype(jnp.float32).reshape(b, t, 2, ML_HEADS) + p['ml_ig_b']
        fg = pieces['ml_f'].astype(jnp.float32).reshape(b, t, 2, ML_HEADS) + p['ml_fg_b']
        return (q, k, v, ig[:, :, 0], fg[:, :, 0]), (q, k, v, ig[:, :, 1], fg[:, :, 1])

    c_f, c_b = prep(pc)
    l_f, l_b = prep(pl)
    b = pc['ml_q'].shape[0]
    state0 = (jnp.zeros((b, ML_HEADS, ML_HD, ML_HD), jnp.float32), jnp.zeros((b, ML_HEADS, ML_HD), jnp.float32), jnp.full((b, ML_HEADS), ML_M_INIT, jnp.float32))
    h_c, h_l = bidirectional(mlstm_scan, mlstm_scan, c_f, l_f, c_b, l_b, state0)
    norm_w = p['ml_norm_w'].reshape(ML_HEADS, ML_HD)

    def post(hh, pieces):
        b_, t_ = hh.shape[:2]
        hn = hh * lax.rsqrt(jnp.mean(hh * hh, axis=-1, keepdims=True) + NORM_EPS) * norm_w
        return hn.reshape(b_, t_, ML_WIDTH) * jax.nn.sigmoid(pieces['ml_o'])

    return post(h_c, pc), post(h_l, pl)


def _affine_combine(e1, e2):
    a1, b1 = e1
    a2, b2 = e2
    return a1 * a2, a2 * b1 + b2


def lru_scan(inputs, h0, gr_w, gr_b, gi_w, gi_b, lam):
    (u,) = inputs
    b, t, _ = u.shape
    ub = u.reshape(b, t, LRU_BLOCKS, LRU_BD)
    r = jax.nn.sigmoid(jnp.einsum('btnd,nde->btne', ub, gr_w).reshape(b, t, LRU_WIDTH) + gr_b)
    i = jax.nn.sigmoid(jnp.einsum('btnd,nde->btne', ub, gi_w).reshape(b, t, LRU_WIDTH) + gi_b)
    log_a = -LRU_C * r * jax.nn.softplus(-lam)
    a = jnp.exp(log_a)
    bx = jnp.sqrt(-jnp.expm1(2.0 * log_a)) * (i * u)
    a_cum, h = lax.associative_scan(_affine_combine, (a, bx), axis=1)
    h = h + a_cum * h0[:, None, :]
    return h, h[:, -1]


def rglru_branch(pc, pl, p):
    u_c = dwconv(pc['lru_x'], p['lru_conv_w'], p['lru_conv_b']).astype(jnp.float32)
    u_l = dwconv(pl['lru_x'], p['lru_conv_w'], p['lru_conv_b']).astype(jnp.float32)
    scan_f = functools.partial(lru_scan, gr_w=p['lru_gr_w'][0], gr_b=p['lru_gr_b'][0], gi_w=p['lru_gi_w'][0], gi_b=p['lru_gi_b'][0], lam=p['lru_lambda'][0])
    scan_b = functools.partial(lru_scan, gr_w=p['lru_gr_w'][1], gr_b=p['lru_gr_b'][1], gi_w=p['lru_gi_w'][1], gi_b=p['lru_gi_b'][1], lam=p['lru_lambda'][1])
    h0 = jnp.zeros((u_c.shape[0], LRU_WIDTH), jnp.float32)
    h_c, h_l = bidirectional(scan_f, scan_b, (u_c,), (u_l,), (u_c,), (u_l,), h0)
    return h_c * jax.nn.gelu(pc['lru_gate']), h_l * jax.nn.gelu(pl['lru_gate'])


def rwkv7_scan(inputs, state0):
    xs = tuple(jnp.moveaxis(a, 1, 0) for a in inputs)

    def step(s, x_t):
        r_t, w_t, k_t, v_t, kap_t, kapa_t = x_t
        s_kap = jnp.einsum('bhvk,bhk->bhv', s, kap_t)
        s = s * w_t[:, :, None, :] - s_kap[..., None] * kapa_t[:, :, None, :] + v_t[..., None] * k_t[:, :, None, :]
        return s, jnp.einsum('bhvk,bhk->bhv', s, r_t)

    state, ys = lax.scan(step, state0, xs)
    return jnp.moveaxis(ys, 0, 1), state


def rwkv7_branch(pc, pl, p):
    r_k = p['rw_r_k'].reshape(RW_HEADS, RW_HD)
    ln_w = p['rw_ln_w'].reshape(RW_HEADS, RW_HD)
    ln_b = p['rw_ln_b'].reshape(RW_HEADS, RW_HD)

    def prep(pieces):
        seg = token_shift(pieces['rw'].astype(jnp.float32), p['rw_mu'])
        r, k, v, dd, ad, gd = split_cols(seg, RW_SPLITS)
        b, t = r.shape[:2]
        heads = lambda a: a.reshape(b, t, RW_HEADS, RW_HD)
        kk = heads(k * p['rw_k_k'])
        kap = kk / jnp.maximum(jnp.sqrt(jnp.sum(kk * kk, axis=-1, keepdims=True)), 1e-12)
        per_dir = []
        for d in range(2):
            dd_d = dd[..., d * RW_DECAY_RANK:(d + 1) * RW_DECAY_RANK]
            ad_d = ad[..., d * RW_ICLR_RANK:(d + 1) * RW_ICLR_RANK]
            w = jnp.exp(-RW_DECAY_SCALE * jax.nn.sigmoid(p['rw_decay0'][d] + jnp.tanh(dd_d) @ p['rw_decay_up'][d]))
            a = jax.nn.sigmoid(p['rw_iclr0'][d] + ad_d @ p['rw_iclr_up'][d])
            kt = heads(k * (1 + (a - 1) * p['rw_k_a']))
            per_dir.append((heads(r), heads(w), kt, heads(v), kap, kap * heads(a)))
        g = jax.nn.sigmoid(gd) @ p['rw_gate_up']
        return per_dir, g

    def post(y, per_dir, g):
        r, kt_f, v = per_dir[0][0], per_dir[0][2], per_dir[0][3]
        kt_b = per_dir[1][2]
        mu = jnp.mean(y, axis=-1, keepdims=True)
        var = jnp.mean(jnp.square(y - mu), axis=-1, keepdims=True)
        yn = (y - mu) * lax.rsqrt(var + RW_LN_EPS) * ln_w + ln_b
        bonus = jnp.sum(r * (kt_f + kt_b) * r_k, axis=-1, keepdims=True) * v
        b, t = y.shape[:2]
        return (yn + bonus).reshape(b, t, RW_WIDTH) * g

    dirs_c, g_c = prep(pc)
    dirs_l, g_l = prep(pl)
    state0 = jnp.zeros((g_c.shape[0], RW_HEADS, RW_HD, RW_HD), jnp.float32)
    y_c, y_l = bidirectional(rwkv7_scan, rwkv7_scan, dirs_c[0], dirs_l[0], dirs_c[1], dirs_l[1], state0)
    return post(y_c, dirs_c, g_c), post(y_l, dirs_l, g_l)


def merge_branches(gates, y_ml, y_lru, y_rw, p):
    g_ml, g_lru, g_rw = jnp.split(gates, N_BRANCHES, axis=-1)
    y = (jax.nn.sigmoid(g_ml) * (y_ml @ p['out_ml']) + jax.nn.sigmoid(g_lru) * (y_lru @ p['out_lru']) + jax.nn.sigmoid(g_rw) * (y_rw @ p['out_rw']))
    return y @ p['w_out']


def hybrid_mixer(hc, hl, p, need_ctx):
    pc = dict(zip(IN_NAMES, split_cols(hc @ p['w_in'], IN_SPLITS)))
    pl = dict(zip(IN_NAMES, split_cols(hl @ p['w_in'], IN_SPLITS)))
    ml_c, ml_l = mlstm_branch(pc, pl, p)
    lru_c, lru_l = rglru_branch(pc, pl, p)
    rw_c, rw_l = rwkv7_branch(pc, pl, p)
    y_l = merge_branches(pl['merge'], ml_l, lru_l, rw_l, p).astype(hl.dtype)
    y_c = merge_branches(pc['merge'], ml_c, lru_c, rw_c, p).astype(hc.dtype) if need_ctx else None
    return y_c, y_l


def swiglu(h, w_in, w_out):
    gate, up = jnp.split(h @ w_in, 2, axis=-1)
    return (jax.nn.silu(gate) * up) @ w_out


def setup_inputs(seed: int = 0) -> dict:
    key = jax.random.key(seed)
    ks = jax.random.split(key, 37)
    L, D = DEPTH, D_MODEL

    def nrm(i, shape, scale):
        return scale * jax.random.normal(ks[i], shape, jnp.float32)

    u = jax.random.uniform(ks[18], (L, 2, LRU_WIDTH), jnp.float32, 0.9, 0.999)
    a_base = u ** (1.0 / LRU_C)
    return {
        'x': nrm(0, (BATCH, SEQ, D), 1.0),
        'c': nrm(1, (BATCH, D), 1.0),
        'ctx': nrm(2, (BATCH, CTX_LEN, D), 1.0),
        'c_ctx': nrm(3, (D,), 1.0),
        'w_mod': nrm(4, (L, D, 6 * D), 0.5 * D ** -0.5),
        'b_mod': nrm(5, (L, 6 * D), 0.01),
        'norm1_w': 1.0 + nrm(6, (L, D), 0.05),
        'norm2_w': 1.0 + nrm(7, (L, D), 0.05),
        'w_in': nrm(8, (L, D, IN_WIDTH), D ** -0.5),
        'ml_ig_b': nrm(9, (L, 2, ML_HEADS), 0.1),
        'ml_fg_b': jnp.linspace(3.0, 6.0, ML_HEADS, dtype=jnp.float32) + nrm(10, (L, 2, ML_HEADS), 0.1),
        'ml_norm_w': 1.0 + nrm(11, (L, ML_WIDTH), 0.05),
        'lru_conv_w': nrm(12, (L, LRU_CONV, LRU_WIDTH), LRU_CONV ** -0.5),
        'lru_conv_b': nrm(13, (L, LRU_WIDTH), 0.01),
        'lru_gr_w': nrm(14, (L, 2, LRU_BLOCKS, LRU_BD, LRU_BD), LRU_BD ** -0.5),
        'lru_gr_b': nrm(15, (L, 2, LRU_WIDTH), 0.01),
        'lru_gi_w': nrm(16, (L, 2, LRU_BLOCKS, LRU_BD, LRU_BD), LRU_BD ** -0.5),
        'lru_gi_b': nrm(17, (L, 2, LRU_WIDTH), 0.01),
        'lru_lambda': jnp.log(a_base) - jnp.log1p(-a_base),
        'rw_mu': jax.random.uniform(ks[19], (L, RW_SEG), jnp.float32),
        'rw_decay0': -1.0 + nrm(20, (L, 2, RW_WIDTH), 0.5),
        'rw_decay_up': nrm(21, (L, 2, RW_DECAY_RANK, RW_WIDTH), 0.1),
        'rw_iclr0': nrm(22, (L, 2, RW_WIDTH), 0.1),
        'rw_iclr_up': nrm(23, (L, 2, RW_ICLR_RANK, RW_WIDTH), 0.5 * RW_ICLR_RANK ** -0.5),
        'rw_gate_up': nrm(24, (L, RW_GATE_RANK, RW_WIDTH), RW_GATE_RANK ** -0.5),
        'rw_k_k': 0.85 + nrm(25, (L, RW_WIDTH), 0.05),
        'rw_k_a': 1.0 + nrm(26, (L, RW_WIDTH), 0.05),
        'rw_r_k': nrm(27, (L, RW_WIDTH), 0.1),
        'rw_ln_w': 1.0 + nrm(28, (L, RW_WIDTH), 0.05),
        'rw_ln_b': nrm(29, (L, RW_WIDTH), 0.01),
        'out_ml': nrm(30, (L, ML_WIDTH, D), ML_WIDTH ** -0.5),
        'out_lru': nrm(31, (L, LRU_WIDTH, D), LRU_WIDTH ** -0.5),
        'out_rw': nrm(32, (L, RW_WIDTH, D), RW_WIDTH ** -0.5),
        'w_out': nrm(33, (L, D, D), D ** -0.5),
        'w_ffn_in': nrm(34, (L, D, 2 * FFN_HIDDEN), D ** -0.5),
        'w_ffn_out': nrm(35, (L, FFN_HIDDEN, D), FFN_HIDDEN ** -0.5),
        'final_norm_w': 1.0 + nrm(36, (D,), 0.05),
    }


def reference(x, c, ctx, c_ctx, w_mod, b_mod, norm1_w, norm2_w, w_in, ml_ig_b, ml_fg_b, ml_norm_w, lru_conv_w, lru_conv_b, lru_gr_w, lru_gr_b, lru_gi_w, lru_gi_b, lru_lambda, rw_mu, rw_decay0, rw_decay_up, rw_iclr0, rw_iclr_up, rw_gate_up, rw_k_k, rw_k_a, rw_r_k, rw_ln_w, rw_ln_b, out_ml, out_lru, out_rw, w_out, w_ffn_in, w_ffn_out, final_norm_w):
    rows = x.shape[1] // GRID_W
    x_lat, x_ctx = x, ctx
    cond_ctx = c_ctx[None, :]
    for layer in range(DEPTH):
        last = layer == DEPTH - 1
        p = {
            'w_in': w_in[layer], 'ml_ig_b': ml_ig_b[layer], 'ml_fg_b': ml_fg_b[layer], 'ml_norm_w': ml_norm_w[layer],
            'lru_conv_w': lru_conv_w[layer], 'lru_conv_b': lru_conv_b[layer], 'lru_gr_w': lru_gr_w[layer],
            'lru_gr_b': lru_gr_b[layer], 'lru_gi_w': lru_gi_w[layer], 'lru_gi_b': lru_gi_b[layer],
            'lru_lambda': lru_lambda[layer], 'rw_mu': rw_mu[layer], 'rw_decay0': rw_decay0[layer],
            'rw_decay_up': rw_decay_up[layer], 'rw_iclr0': rw_iclr0[layer], 'rw_iclr_up': rw_iclr_up[layer],
            'rw_gate_up': rw_gate_up[layer], 'rw_k_k': rw_k_k[layer], 'rw_k_a': rw_k_a[layer], 'rw_r_k': rw_r_k[layer],
            'rw_ln_w': rw_ln_w[layer], 'rw_ln_b': rw_ln_b[layer], 'out_ml': out_ml[layer], 'out_lru': out_lru[layer],
            'out_rw': out_rw[layer], 'w_out': w_out[layer],
        }
        sh1, sc1, g1, sh2, sc2, g2 = adaln(c, w_mod[layer], b_mod[layer])
        csh1, csc1, cg1, csh2, csc2, cg2 = adaln(cond_ctx, w_mod[layer], b_mod[layer])
        h_lat = modulate(rms_norm(x_lat, norm1_w[layer]), sh1, sc1)
        h_ctx = modulate(rms_norm(x_ctx, norm1_w[layer]), csh1, csc1)
        column_order = layer % 2 == 1
        if column_order:
            h_lat = grid_transpose(h_lat, rows, GRID_W)
        y_ctx, y_lat = hybrid_mixer(h_ctx, h_lat, p, not last)
        if column_order:
            y_lat = grid_transpose(y_lat, GRID_W, rows)
        x_lat = x_lat + g1 * y_lat
        x_lat = x_lat + g2 * swiglu(modulate(rms_norm(x_lat, norm2_w[layer]), sh2, sc2), w_ffn_in[layer], w_ffn_out[layer])
        if not last:
            x_ctx = x_ctx + cg1 * y_ctx
            x_ctx = x_ctx + cg2 * swiglu(modulate(rms_norm(x_ctx, norm2_w[layer]), csh2, csc2), w_ffn_in[layer], w_ffn_out[layer])
    return rms_norm(x_lat, final_norm_w)
```

```python
import functools
import math

import jax
import jax.numpy as jnp
from jax import lax
from jax.experimental import pallas as pl
from jax.experimental.pallas import tpu as pltpu

F32 = jnp.float32
BF16 = jnp.bfloat16

NORM_EPS = 1e-6
GRID_W = 64
TOKEN_BLOCK = 256
HALO = 8

ML_HEADS = 4
ML_HD = 256
ML_CHUNK = 64
ML_M_INIT = -1e30

LRU_BLOCKS = 16
LRU_BD = 64
LRU_CONV = 4
LRU_C = 8.0
LRU_GROUP = 256

RW_HD = 64
RW_HEADS = 16
RW_PAIRS = 8
RW_CHUNK = 128
RW_RANK = 128
RW_DECAY_SCALE = math.exp(-0.5)
RW_LN_EPS = 64e-5

VMEM_LIMIT = 56 * 1024 * 1024


def _cparams(*sem):
    return pltpu.CompilerParams(dimension_semantics=sem, vmem_limit_bytes=VMEM_LIMIT)


def _dot(a, b):
    return jnp.dot(a, b, preferred_element_type=F32)


def _dot_nt(a, b):
    return lax.dot_general(a, b, (((1,), (1,)), ((), ())), preferred_element_type=F32)


def _dot_tn(a, b):
    return lax.dot_general(a, b, (((0,), (0,)), ((), ())), preferred_element_type=F32)


def _split3(x):
    h = x.astype(BF16)
    r = x - h.astype(F32)
    m = r.astype(BF16)
    l = (r - m.astype(F32)).astype(BF16)
    return h, m, l


def _tri_sum_left(tri, x):
    w = x.shape[1]
    h, m, l = _split3(x)
    y = _dot(tri, jnp.concatenate([h, m, l], axis=1))
    return (y[:, :w] + y[:, w:2 * w]) + y[:, 2 * w:]


def _tri_sum_right(x, tri):
    r = x.shape[0]
    h, m, l = _split3(x)
    y = _dot(jnp.concatenate([h, m, l], axis=0), tri)
    return (y[:r] + y[r:2 * r]) + y[2 * r:]


def _group_sum(x, ones_bd):
    n = x.shape[0]
    h = x.astype(BF16)
    l = (x - h.astype(F32)).astype(BF16)
    y = _dot(jnp.concatenate([h, l], axis=0), ones_bd)
    return y[:n] + y[n:]


def _bwd_block(j, nc, nb):
    return jnp.where(j < nc, nc - 1 - j, nb + nc - 1 - j)


def _block_fn(reverse, nc, nb):
    if reverse:
        return lambda j: _bwd_block(j, nc, nb)
    return lambda j: j


def _mod_kernel(c_ref, w_ref, b_ref, o_ref):
    c = c_ref[...]
    s = (c * jax.nn.sigmoid(c)).astype(BF16)
    o_ref[0] = _dot(s, w_ref[0].astype(BF16)) + b_ref[0]


def _adaln_all(cond, w_mod, b_mod):
    L, D, D6 = w_mod.shape
    BP = cond.shape[0]
    tn = 1536
    return pl.pallas_call(
        _mod_kernel,
        grid=(L, D6 // tn),
        in_specs=[
            pl.BlockSpec((BP, D), lambda l, n: (0, 0)),
            pl.BlockSpec((1, D, tn), lambda l, n: (l, 0, n)),
            pl.BlockSpec((1, 1, tn), lambda l, n: (l, 0, n)),
        ],
        out_specs=pl.BlockSpec((1, BP, tn), lambda l, n: (l, 0, n)),
        out_shape=jax.ShapeDtypeStruct((L, BP, D6), F32),
        compiler_params=_cparams("arbitrary", "arbitrary"),
    )(cond, w_mod, b_mod.reshape(L, 1, D6))


def _rms(x, w):
    return x * lax.rsqrt(jnp.mean(x * x, axis=-1, keepdims=True) + NORM_EPS) * w


def _prenorm_kernel(x_ref, w_ref, mod_ref, h_ref):
    m = mod_ref[0]
    y = _rms(x_ref[0], w_ref[...])
    h_ref[0] = (y * (1.0 + m[1:2]) + m[0:1]).astype(BF16)


def _prenorm(x, norm_w, mod, nc):
    B, TA, D = x.shape
    nb = TA // TOKEN_BLOCK
    return pl.pallas_call(
        _prenorm_kernel,
        grid=(B, nb),
        in_specs=[
            pl.BlockSpec((1, TOKEN_BLOCK, D), lambda b, j: (b, j, 0)),
            pl.BlockSpec((1, D), lambda b, j: (0, 0)),
            pl.BlockSpec((1, 6, D), lambda b, j: (jnp.where(j < nc, B, b), 0, 0)),
        ],
        out_specs=pl.BlockSpec((1, TOKEN_BLOCK, D), lambda b, j: (b, j, 0)),
        out_shape=jax.ShapeDtypeStruct((B, TA, D), BF16),
        compiler_params=_cparams("arbitrary", "arbitrary"),
    )(x, norm_w.reshape(1, D), mod)


def _final_norm_kernel(x_ref, w_ref, o_ref):
    o_ref[0] = _rms(x_ref[0], w_ref[...])


def _final_norm(x, w, nc):
    B, TA, D = x.shape
    nl = TA // TOKEN_BLOCK - nc
    return pl.pallas_call(
        _final_norm_kernel,
        grid=(B, nl),
        in_specs=[
            pl.BlockSpec((1, TOKEN_BLOCK, D), lambda b, j: (b, j + nc, 0)),
            pl.BlockSpec((1, D), lambda b, j: (0, 0)),
        ],
        out_specs=pl.BlockSpec((1, TOKEN_BLOCK, D), lambda b, j: (b, j, 0)),
        out_shape=jax.ShapeDtypeStruct((B, nl * TOKEN_BLOCK, D), F32),
        compiler_params=_cparams("arbitrary", "arbitrary"),
    )(x, w.reshape(1, D))


def _mm_kernel(h_ref, w_ref, o_ref):
    o_ref[...] = _dot(h_ref[...], w_ref[...]).astype(o_ref.dtype)


def _pick(n, cands):
    for c in cands:
        if n % c == 0:
            return c
    raise ValueError(f"no block size for {n}")


def _matmul(h, w, out_dtype, tn):
    N, K = h.shape
    M = w.shape[1]
    tm = _pick(N, (1024, 512, 256))
    return pl.pallas_call(
        _mm_kernel,
        grid=(M // tn, N // tm),
        in_specs=[
            pl.BlockSpec((tm, K), lambda n, m: (m, 0)),
            pl.BlockSpec((K, tn), lambda n, m: (0, n)),
        ],
        out_specs=pl.BlockSpec((tm, tn), lambda n, m: (m, n)),
        out_shape=jax.ShapeDtypeStruct((N, M), out_dtype),
        compiler_params=_cparams("arbitrary", "arbitrary"),
    )(h, w)


def _mlstm_kernel(reverse, *refs):
    if reverse:
        (q_ref, k_ref, v_ref, gc_ref, gr_ref, bc_ref, br_ref, yf_ref, o_ref, nw_ref,
         out_ref, ct_ref, n_ref, m_ref) = refs
    else:
        (q_ref, k_ref, v_ref, gc_ref, gr_ref, bc_ref, br_ref,
         out_ref, ct_ref, n_ref, m_ref) = refs
    C = ML_CHUNK
    j = pl.program_id(1)

    @pl.when(j == 0)
    def _():
        ct_ref[...] = jnp.zeros_like(ct_ref)
        n_ref[...] = jnp.zeros_like(n_ref)
        m_ref[...] = jnp.full_like(m_ref, ML_M_INIT)

    row = lax.broadcasted_iota(jnp.int32, (C, C), 0)
    col = lax.broadcasted_iota(jnp.int32, (C, C), 1)
    causal = (col >= row) if reverse else (col <= row)
    tri_l = causal.astype(BF16)
    tri_r = ((row >= col) if reverse else (row <= col)).astype(BF16)
    last = 0 if reverse else C - 1
    d = 1 if reverse else 0

    n_chunks = TOKEN_BLOCK // C
    order = range(n_chunks - 1, -1, -1) if reverse else range(n_chunks)
    for c in order:
        t0 = c * C
        gc = gc_ref[0, t0:t0 + C, :] + bc_ref[...]
        gr = gr_ref[0, c] + br_ref[...]
        fc = _tri_sum_left(tri_l, jax.nn.log_sigmoid(gc))
        fr = _tri_sum_right(jax.nn.log_sigmoid(gr), tri_r)
        for h in range(ML_HEADS):
            gi = d * ML_HEADS + h
            i_c = gc[:, gi:gi + 1]
            f_c = fc[:, 8 + gi:9 + gi]
            i_r = gr[gi:gi + 1, :]
            f_r = fr[8 + gi:9 + gi, :]
            f_end = f_c[last:last + 1, :]
            m_st = m_ref[h][0:1, 0:1]
            logw = jnp.where(causal, f_c - f_r + i_r, -jnp.inf)
            inter = f_c + m_st
            m_t = jnp.maximum(jnp.max(logw, axis=1, keepdims=True), inter)
            w_intra = jnp.exp(logw - m_t)
            w_inter = jnp.exp(inter - m_t)
            sl = slice(h * ML_HD, (h + 1) * ML_HD)
            q = q_ref[0, t0:t0 + C, sl]
            k = k_ref[0, t0:t0 + C, sl]
            v = v_ref[0, t0:t0 + C, sl]
            s = _dot_nt(q, k) * w_intra
            ct = ct_ref[h]
            nv = n_ref[h][0:1, :]
            num = _dot(s.astype(BF16), v) + w_inter * _dot(q, ct.astype(BF16))
            den = jnp.sum(s, axis=1, keepdims=True) + w_inter * jnp.sum(q.astype(F32) * nv, axis=1, keepdims=True)
            h_out = num / jnp.maximum(jnp.abs(den), jnp.exp(-m_t))
            log_end_c = f_end - f_c + i_c
            log_end_r = f_end - f_r + i_r
            m_new = jnp.maximum(f_end + m_st, jnp.max(log_end_r, axis=1, keepdims=True))
            w_end = jnp.exp(log_end_c - m_new)
            decay = jnp.exp(f_end + m_st - m_new)
            wv = (w_end * v.astype(F32)).astype(BF16)
            ct_ref[h] = decay * ct + _dot_tn(k, wv)
            n_new = decay * nv + jnp.sum(w_end * k.astype(F32), axis=0, keepdims=True)
            n_ref[h] = jnp.broadcast_to(n_new, n_ref.shape[1:])
            m_ref[h] = jnp.broadcast_to(m_new, m_ref.shape[1:])
            if reverse:
                tot = yf_ref[0, t0:t0 + C, sl] + h_out
                hn = tot * lax.rsqrt(jnp.mean(tot * tot, axis=-1, keepdims=True) + NORM_EPS) * nw_ref[:, sl]
                out_ref[0, t0:t0 + C, sl] = (hn * jax.nn.sigmoid(o_ref[0, t0:t0 + C, sl])).astype(out_ref.dtype)
            else:
                out_ref[0, t0:t0 + C, sl] = h_out


def _mlstm(qkv, gcol, grow, bcol, brow, o, norm_w, nc):
    B, TA, W3 = qkv.shape
    W = W3 // 3
    nb = TA // TOKEN_BLOCK
    cpb = TOKEN_BLOCK // ML_CHUNK
    scratch = [
        pltpu.VMEM((ML_HEADS, ML_HD, ML_HD), F32),
        pltpu.VMEM((ML_HEADS, 8, ML_HD), F32),
        pltpu.VMEM((ML_HEADS, 8, 128), F32),
    ]

    def run(reverse, extra_in, extra_specs, out_dtype):
        blk = _block_fn(reverse, nc, nb)
        tok = lambda w, cb: pl.BlockSpec((1, TOKEN_BLOCK, w), lambda b, j: (b, blk(j), cb))
        in_specs = [
            tok(W, 0), tok(W, 1), tok(W, 2),
            pl.BlockSpec((1, TOKEN_BLOCK, 128), lambda b, j: (b, blk(j), 0)),
            pl.BlockSpec((1, cpb, 16, ML_CHUNK), lambda b, j: (b, blk(j), 0, 0)),
            pl.BlockSpec((1, 128), lambda b, j: (0, 0)),
            pl.BlockSpec((16, 1), lambda b, j: (0, 0)),
        ] + extra_specs(tok)
        return pl.pallas_call(
            functools.partial(_mlstm_kernel, reverse),
            grid=(B, nb),
            in_specs=in_specs,
            out_specs=tok(W, 0),
            out_shape=jax.ShapeDtypeStruct((B, TA, W), out_dtype),
            scratch_shapes=scratch,
            compiler_params=_cparams("arbitrary", "arbitrary"),
        )(qkv, qkv, qkv, gcol, grow, bcol, brow, *extra_in)

    yf = run(False, (), lambda tok: [], F32)
    return run(True, (yf, o, norm_w.reshape(1, W)),
               lambda tok: [tok(W, 0), tok(W, 0), pl.BlockSpec((1, W), lambda b, j: (0, 0))], BF16)


def _halo_specs(width, blk, TA):
    per = TOKEN_BLOCK // HALO
    last = TA // HALO - 1
    prev = pl.BlockSpec((1, HALO, width), lambda b, j: (b, jnp.maximum(blk(j) * per - 1, 0), 0))
    nxt = pl.BlockSpec((1, HALO, width), lambda b, j: (b, jnp.minimum((blk(j) + 1) * per, last), 0))
    return prev, nxt


def _neg_expm1(z):
    t = jnp.tanh(0.5 * z)
    return -2.0 * t / (1.0 - t)


def _lru_kernel(reverse, nc, nb, *refs):
    if reverse:
        (x_ref, xp_ref, xn_ref, cw_ref, cb_ref, wr_ref, wi_ref, br_ref, bi_ref, lam_ref,
         hf_ref, gate_ref, out_ref, xpad_s, a_s, b_s, h_s) = refs
    else:
        (x_ref, xp_ref, xn_ref, cw_ref, cb_ref, wr_ref, wi_ref, br_ref, bi_ref, lam_ref,
         out_ref, xpad_s, a_s, b_s, h_s) = refs
    TB = TOKEN_BLOCK
    j = pl.program_id(1)
    blk = _bwd_block(j, nc, nb) if reverse else j

    @pl.when(j == 0)
    def _():
        h_s[...] = jnp.zeros_like(h_s)

    seg_start = (blk == 0) | (blk == nc)
    seg_end = (blk == nc - 1) | (blk == nb - 1)
    xpad_s[0:HALO, :] = jnp.where(seg_start, 0.0, xp_ref[0])
    xpad_s[HALO:HALO + TB, :] = x_ref[0]
    xpad_s[HALO + TB:, :] = jnp.where(seg_end, 0.0, xn_ref[0])
    left = LRU_CONV // 2
    u = cb_ref[...]
    for tap in range(LRU_CONV):
        off = HALO - left + tap
        u = u + cw_ref[tap:tap + 1, :] * xpad_s[off:off + TB, :]
    ub = u.astype(BF16)
    W = u.shape[1]
    r_parts, i_parts = [], []
    for g in range(W // LRU_GROUP):
        ug = ub[:, g * LRU_GROUP:(g + 1) * LRU_GROUP]
        r_parts.append(_dot(ug, wr_ref[g]))
        i_parts.append(_dot(ug, wi_ref[g]))
    r = jax.nn.sigmoid(jnp.concatenate(r_parts, axis=1) + br_ref[...])
    i = jax.nn.sigmoid(jnp.concatenate(i_parts, axis=1) + bi_ref[...])
    log_a = -LRU_C * r * jax.nn.softplus(-lam_ref[...])
    a_s[...] = jnp.exp(log_a)
    b_s[...] = jnp.sqrt(_neg_expm1(2.0 * log_a)) * (i * u)

    def step(t, h):
        idx = (TB - 1 - t) if reverse else t
        h = a_s[pl.ds(idx, 1), :] * h + b_s[pl.ds(idx, 1), :]
        b_s[pl.ds(idx, 1), :] = h
        return h

    h_s[...] = lax.fori_loop(0, TB, step, h_s[...], unroll=8)
    if reverse:
        out_ref[0] = ((hf_ref[0] + b_s[...]) * jax.nn.gelu(gate_ref[0])).astype(out_ref.dtype)
    else:
        out_ref[0] = b_s[...]


def _lru(x, gate, cw, cb, wr, wi, br, bi, lam, nc):
    B, TA, W = x.shape
    nb = TA // TOKEN_BLOCK
    ng = W // LRU_GROUP
    scratch = [
        pltpu.VMEM((TOKEN_BLOCK + 2 * HALO, W), F32),
        pltpu.VMEM((TOKEN_BLOCK, W), F32),
        pltpu.VMEM((TOKEN_BLOCK, W), F32),
        pltpu.VMEM((1, W), F32),
    ]

    def run(d, extra_in, out_dtype):
        reverse = d == 1
        blk = _block_fn(reverse, nc, nb)
        tok = pl.BlockSpec((1, TOKEN_BLOCK, W), lambda b, j: (b, blk(j), 0))
        prev, nxt = _halo_specs(W, blk, TA)
        const = lambda shape: pl.BlockSpec(shape, lambda b, j: (0,) * len(shape))
        in_specs = [tok, prev, nxt, const((LRU_CONV, W)), const((1, W)),
                    const((ng, LRU_GROUP, LRU_GROUP)), const((ng, LRU_GROUP, LRU_GROUP)),
                    const((1, W)), const((1, W)), const((1, W))] + [tok] * len(extra_in)
        return pl.pallas_call(
            functools.partial(_lru_kernel, reverse, nc, nb),
            grid=(B, nb),
            in_specs=in_specs,
            out_specs=tok,
            out_shape=jax.ShapeDtypeStruct((B, TA, W), out_dtype),
            scratch_shapes=scratch,
            compiler_params=_cparams("arbitrary", "arbitrary"),
        )(x, x, x, cw, cb.reshape(1, W), wr[d], wi[d], br[d:d + 1], bi[d:d + 1], lam[d:d + 1], *extra_in)

    hf = run(0, (), F32)
    return run(1, (hf, gate), BF16)


def _rwkv_kernel(reverse, nc, nb, *refs):
    if reverse:
        (x_ref, xp_ref, xn_ref, mu_ref, kk_ref, ka_ref, d0_ref, du_ref, a0_ref, au_ref,
         a0f_ref, auf_ref, gu_ref, rk_ref, lnw_ref, lnb_ref, yf_ref,
         out_ref, xpad_s, r_s, kap_s, v_s, kt_s, b_s, lw_s, y_s, bon_s, g_s, st_s) = refs
    else:
        (x_ref, xp_ref, xn_ref, mu_ref, kk_ref, ka_ref, d0_ref, du_ref, a0_ref, au_ref,
         out_ref, xpad_s, r_s, kap_s, v_s, kt_s, b_s, lw_s, st_s) = refs
    TB = TOKEN_BLOCK
    C = RW_CHUNK
    P = 2 * RW_HD
    D = RW_HEADS * RW_HD
    j = pl.program_id(1)
    blk = _bwd_block(j, nc, nb) if reverse else j

    @pl.when(j == 0)
    def _():
        st_s[...] = jnp.zeros_like(st_s)

    seg_start = (blk == 0) | (blk == nc)
    seg_end = (blk == nc - 1) | (blk == nb - 1)
    xpad_s[0:HALO, :] = jnp.where(seg_start, 0.0, xp_ref[0])
    xpad_s[HALO:HALO + TB, :] = x_ref[0]
    xpad_s[HALO + TB:, :] = jnp.where(seg_end, 0.0, xn_ref[0])
    x = x_ref[0]
    nbr = 0.5 * (xpad_s[HALO - 1:HALO - 1 + TB, :] + xpad_s[HALO + 1:HALO + 1 + TB, :])
    seg = x + mu_ref[...] * (nbr - x)
    r = seg[:, 0:D]
    k = seg[:, D:2 * D]
    v = seg[:, 2 * D:3 * D]
    dd = seg[:, 3 * D:3 * D + RW_RANK]
    ad = seg[:, 3 * D + RW_RANK:3 * D + 2 * RW_RANK]
    gd = seg[:, 3 * D + 2 * RW_RANK:3 * D + 3 * RW_RANK]

    lane = lax.broadcasted_iota(jnp.int32, (P, P), 1)
    sub = lax.broadcasted_iota(jnp.int32, (P, P), 0)
    same_head = (lane < RW_HD) == (sub < RW_HD)
    ones_bd = same_head.astype(BF16)

    logw = -RW_DECAY_SCALE * jax.nn.sigmoid(d0_ref[...] + _dot(jnp.tanh(dd).astype(BF16), du_ref[...]))
    ad_b = ad.astype(BF16)
    a = jax.nn.sigmoid(a0_ref[...] + _dot(ad_b, au_ref[...]))
    kk = k * kk_ref[...]
    kt = k * (1.0 + (a - 1.0) * ka_ref[...])
    if reverse:
        a_f = jax.nn.sigmoid(a0f_ref[...] + _dot(ad_b, auf_ref[...]))
        bon = r * (kt + k * (1.0 + (a_f - 1.0) * ka_ref[...])) * rk_ref[...]
        g_out = _dot(jax.nn.sigmoid(gd).astype(BF16), gu_ref[...])
    for p in range(RW_PAIRS):
        sl = slice(p * P, (p + 1) * P)
        if reverse:
            bon_s[p] = bon[:, sl]
            g_s[p] = g_out[:, sl]
        kkp = kk[:, sl]
        nrm = jnp.maximum(jnp.sqrt(_group_sum(kkp * kkp, ones_bd)), 1e-12)
        kap = kkp / nrm
        r_s[p] = r[:, sl]
        v_s[p] = v[:, sl]
        kap_s[p] = kap
        kt_s[p] = kt[:, sl]
        b_s[p] = kap * a[:, sl]
        lw_s[p] = logw[:, sl]

    row = lax.broadcasted_iota(jnp.int32, (C, C), 0)
    col = lax.broadcasted_iota(jnp.int32, (C, C), 1)
    incl = (col >= row) if reverse else (col <= row)
    strict = (col > row) if reverse else (col < row)
    tri = incl.astype(BF16)
    eye = (row == col).astype(F32)
    merge_masks = []
    for i in range(int(math.log2(C))):
        same_parent = jnp.right_shift(row, i + 1) == jnp.right_shift(col, i + 1)
        merge_masks.append(same_parent & (jnp.right_shift(row, i) != jnp.right_shift(col, i)))
    lane_lo = lax.broadcasted_iota(jnp.int32, (1, P), 1) < RW_HD
    head_masks = (lane_lo, jnp.logical_not(lane_lo))
    mid = C // 2
    n_chunks = TB // C
    order = range(n_chunks - 1, -1, -1) if reverse else range(n_chunks)

    def pair_body(p, carry):
        for c in order:
            rows = slice(c * C, (c + 1) * C)
            lw = lw_s[p, rows, :]
            rr = r_s[p, rows, :]
            kap = kap_s[p, rows, :]
            vv = v_s[p, rows, :]
            ktc = kt_s[p, rows, :]
            bb = b_s[p, rows, :]
            L = _tri_sum_left(tri, lw)
            Lx = L - lw
            Lm = L[mid:mid + 1, :]
            Lc = jnp.sum(lw, axis=0, keepdims=True)
            kap_i = kap * jnp.exp(Lx - Lm)
            r_i = rr * jnp.exp(L - Lm)
            e_neg = jnp.exp(Lm - L)
            b_i = bb * e_neg
            k_i = ktc * e_neg
            kap_st = (kap * jnp.exp(Lx)).astype(BF16)
            r_st = (rr * jnp.exp(L)).astype(BF16)
            e_end = jnp.exp(Lc - L)
            b_end = (bb * e_end).astype(BF16)
            k_end = (ktc * e_end).astype(BF16)
            vb = vv.astype(BF16)
            xs = jnp.concatenate([kap_i, r_i], axis=0)
            ys = jnp.concatenate([b_i, k_i], axis=0).astype(BF16)
            t_mats, rb_mats, bv_rkv = [], [], []
            for hm in head_masks:
                g = _dot_nt(jnp.where(hm, xs, 0.0).astype(BF16), ys)
                a_m = jnp.where(strict, g[:C, :C], 0.0)
                bm = jnp.where(strict, g[:C, C:], 0.0)
                rb = jnp.where(incl, g[C:, :C], 0.0)
                rk = jnp.where(incl, g[C:, C:], 0.0)
                t = eye - jnp.where(merge_masks[0], a_m, 0.0)
                for mm in merge_masks[1:]:
                    tb = t.astype(BF16)
                    a_off = jnp.where(mm, a_m, 0.0).astype(BF16)
                    t = t - _dot(tb, _dot(a_off, tb).astype(BF16))
                t_mats.append(t)
                rb_mats.append(rb)
                bv_rkv.append(_dot(jnp.concatenate([bm, rk], axis=0).astype(BF16), vb))
            sel = lambda x0, x1: jnp.where(lane_lo, x0, x1)
            bv = sel(bv_rkv[0][:C], bv_rkv[1][:C])
            rkv = sel(bv_rkv[0][C:], bv_rkv[1][C:])
            rhs = jnp.concatenate([kap_st, bv.astype(BF16)], axis=1)
            tk = [_dot(t.astype(BF16), rhs) for t in t_mats]
            w_st = sel(tk[0][:, :P], tk[1][:, :P])
            u_free = sel(tk[0][:, P:], tk[1][:, P:])
            st = st_s[p]
            ws_rs = _dot_nt(jnp.concatenate([w_st.astype(BF16), r_st], axis=0), st.astype(BF16))
            u = -(ws_rs[:C] + u_free)
            ub = u.astype(BF16)
            y = ws_rs[C:] + rkv + sel(_dot(rb_mats[0].astype(BF16), ub), _dot(rb_mats[1].astype(BF16), ub))
            upd = _dot_tn(jnp.concatenate([ub, vb], axis=0), jnp.concatenate([b_end, k_end], axis=0))
            st_s[p] = st * jnp.exp(Lc) + jnp.where(same_head, upd, 0.0)
            if reverse:
                y_s[p, rows, :] = y
            else:
                out_ref[0, p, rows, :] = y
        return carry

    lax.fori_loop(0, RW_PAIRS, pair_body, 0)

    if reverse:
        inv = 1.0 / RW_HD
        for p in range(RW_PAIRS):
            sl = slice(p * P, (p + 1) * P)
            y = yf_ref[0, p] + y_s[p]
            mu = _group_sum(y, ones_bd) * inv
            dlt = y - mu
            var = _group_sum(dlt * dlt, ones_bd) * inv
            yn = dlt * lax.rsqrt(var + RW_LN_EPS) * lnw_ref[:, sl] + lnb_ref[:, sl]
            bonus = _group_sum(bon_s[p], ones_bd) * v_s[p]
            out_ref[0, :, sl] = ((yn + bonus) * g_s[p]).astype(out_ref.dtype)


def _rwkv(x, mu, k_k, k_a, d0, du, a0, au, gate_up, r_k, ln_w, ln_b, nc):
    B, TA, WS = x.shape
    D = RW_HEADS * RW_HD
    P = 2 * RW_HD
    nb = TA // TOKEN_BLOCK
    slab = lambda: pltpu.VMEM((RW_PAIRS, TOKEN_BLOCK, P), F32)
    const = lambda shape: pl.BlockSpec(shape, lambda b, j: (0,) * len(shape))
    row = lambda a: a.reshape(1, -1)

    def run(d, extra_in, extra_specs, out_spec, out_shape, extra_scratch):
        reverse = d == 1
        blk = _block_fn(reverse, nc, nb)
        tok = pl.BlockSpec((1, TOKEN_BLOCK, WS), lambda b, j: (b, blk(j), 0))
        prev, nxt = _halo_specs(WS, blk, TA)
        in_specs = [tok, prev, nxt, const((1, WS)), const((1, D)), const((1, D)),
                    const((1, D)), const((RW_RANK, D)), const((1, D)), const((RW_RANK, D))] + extra_specs(blk)
        scratch = [pltpu.VMEM((TOKEN_BLOCK + 2 * HALO, WS), F32)] + [slab() for _ in range(6)] + extra_scratch + [
            pltpu.VMEM((RW_PAIRS, P, P), F32)]
        return pl.pallas_call(
            functools.partial(_rwkv_kernel, reverse, nc, nb),
            grid=(B, nb),
            in_specs=in_specs,
            out_specs=out_spec(blk),
            out_shape=out_shape,
            scratch_shapes=scratch,
            compiler_params=_cparams("arbitrary", "arbitrary"),
        )(x, x, x, row(mu), row(k_k), row(k_a), d0[d:d + 1], du[d], a0[d:d + 1], au[d], *extra_in)

    pair_spec = lambda blk: pl.BlockSpec((1, RW_PAIRS, TOKEN_BLOCK, P), lambda b, j: (b, 0, blk(j), 0))
    yf = run(0, (), lambda blk: [], pair_spec, jax.ShapeDtypeStruct((B, RW_PAIRS, TA, P), F32), [])
    return run(
        1,
        (a0[0:1], au[0], gate_up, row(r_k), row(ln_w), row(ln_b), yf),
        lambda blk: [const((1, D)), const((RW_RANK, D)), const((RW_RANK, D)), const((1, D)), const((1, D)),
                     const((1, D)), pair_spec(blk)],
        lambda blk: pl.BlockSpec((1, TOKEN_BLOCK, D), lambda b, j: (b, blk(j), 0)),
        jax.ShapeDtypeStruct((B, TA, D), BF16),
        [slab(), slab(), slab()],
    )


def _merge_kernel(x_ref, yml_ref, ylru_ref, yrw_ref, g_ref, oml_ref, olru_ref, orw_ref, wout_ref,
                  mod_ref, nw_ref, xo_ref, h_ref):
    D = x_ref.shape[-1]
    g = g_ref[0]
    acc = jax.nn.sigmoid(g[:, 0:D]) * _dot(yml_ref[0], oml_ref[...])
    acc = acc + jax.nn.sigmoid(g[:, D:2 * D]) * _dot(ylru_ref[0], olru_ref[...])
    acc = acc + jax.nn.sigmoid(g[:, 2 * D:3 * D]) * _dot(yrw_ref[0], orw_ref[...])
    y = _dot(acc.astype(BF16), wout_ref[...])
    m = mod_ref[0]
    xn = x_ref[0] + m[2:3] * y
    xo_ref[0] = xn
    h_ref[0] = (_rms(xn, nw_ref[...]) * (1.0 + m[4:5]) + m[3:4]).astype(BF16)


def _merge(x, yml, ylru, yrw, gates, oml, olru, orw, wout, mod, norm2_w, nc, skip_ctx):
    B, TA, D = x.shape
    nb = TA // TOKEN_BLOCK
    off = nc if skip_ctx else 0
    rows_out = (nb - off) * TOKEN_BLOCK
    tok = lambda w: pl.BlockSpec((1, TOKEN_BLOCK, w), lambda b, j: (b, j + off, 0))
    otok = pl.BlockSpec((1, TOKEN_BLOCK, D), lambda b, j: (b, j, 0))
    wspec = pl.BlockSpec((D, D), lambda b, j: (0, 0))
    return pl.pallas_call(
        _merge_kernel,
        grid=(B, nb - off),
        in_specs=[tok(D), tok(D), tok(D), tok(D), tok(3 * D), wspec, wspec, wspec, wspec,
                  pl.BlockSpec((1, 6, D), lambda b, j: (jnp.where(j + off < nc, B, b), 0, 0)),
                  pl.BlockSpec((1, D), lambda b, j: (0, 0))],
        out_specs=[otok, otok],
        out_shape=[jax.ShapeDtypeStruct((B, rows_out, D), F32), jax.ShapeDtypeStruct((B, rows_out, D), BF16)],
        input_output_aliases={} if skip_ctx else {0: 0},
        compiler_params=_cparams("arbitrary", "arbitrary"),
    )(x, yml, ylru, yrw, gates, oml, olru, orw, wout, mod, norm2_w.reshape(1, D))


def _ffn_kernel(x_ref, h_ref, win_ref, wout_ref, mod_ref, xo_ref):
    F = wout_ref.shape[0]
    u = _dot(h_ref[0], win_ref[...])
    gate = u[:, :F]
    act = (gate * jax.nn.sigmoid(gate) * u[:, F:]).astype(BF16)
    xo_ref[0] = x_ref[0] + mod_ref[0][5:6] * _dot(act, wout_ref[...])


def _ffn(x, h, win, wout, mod, nc):
    B, TA, D = x.shape
    F = wout.shape[0]
    nb = TA // TOKEN_BLOCK
    tok = pl.BlockSpec((1, TOKEN_BLOCK, D), lambda b, j: (b, j, 0))
    return pl.pallas_call(
        _ffn_kernel,
        grid=(B, nb),
        in_specs=[tok, tok,
                  pl.BlockSpec((D, 2 * F), lambda b, j: (0, 0)),
                  pl.BlockSpec((F, D), lambda b, j: (0, 0)),
                  pl.BlockSpec((1, 6, D), lambda b, j: (jnp.where(j < nc, B, b), 0, 0))],
        out_specs=tok,
        out_shape=jax.ShapeDtypeStruct((B, TA, D), F32),
        input_output_aliases={0: 0},
        compiler_params=_cparams("arbitrary", "arbitrary"),
    )(x, h, win, wout, mod)


def _grid_transpose_latent(xa, ctx_len, rows, cols):
    B, TA, D = xa.shape
    lat = xa[:, ctx_len:].reshape(B, rows, cols, D).transpose(0, 2, 1, 3).reshape(B, rows * cols, D)
    return jnp.concatenate([xa[:, :ctx_len], lat], axis=1)


def _block_diag_groups(w):
    per = LRU_GROUP // LRU_BD
    d2 = w.shape[0]
    wg = w.reshape(d2, LRU_BLOCKS // per, per, LRU_BD, LRU_BD)
    eye = jnp.eye(per, dtype=w.dtype)
    out = jnp.einsum('dgpio,pq->dgpiqo', wg, eye)
    return out.reshape(d2, LRU_BLOCKS // per, LRU_GROUP, LRU_GROUP).astype(BF16)


def _pad_dir_rows(w):
    z = jnp.zeros_like(w[0])
    return jnp.stack([jnp.concatenate([w[0], z], axis=0), jnp.concatenate([z, w[1]], axis=0)]).astype(BF16)


def kernel(x, c, ctx, c_ctx, w_mod, b_mod, norm1_w, norm2_w, w_in, ml_ig_b, ml_fg_b, ml_norm_w, lru_conv_w, lru_conv_b, lru_gr_w, lru_gr_b, lru_gi_w, lru_gi_b, lru_lambda, rw_mu, rw_decay0, rw_decay_up, rw_iclr0, rw_iclr_up, rw_gate_up, rw_k_k, rw_k_a, rw_r_k, rw_ln_w, rw_ln_b, out_ml, out_lru, out_rw, w_out, w_ffn_in, w_ffn_out, final_norm_w):
    B, T, D = x.shape
    CTX = ctx.shape[1]
    L = w_mod.shape[0]
    TA = CTX + T
    assert CTX % TOKEN_BLOCK == 0 and T % TOKEN_BLOCK == 0 and T % GRID_W == 0
    nc = CTX // TOKEN_BLOCK
    rows = T // GRID_W
    N = B * TA
    MLW = ML_HEADS * ML_HD

    BP = -(-(B + 1) // 8) * 8
    cond = jnp.concatenate([c, c_ctx[None, :], jnp.zeros((BP - B - 1, D), F32)], axis=0)
    mods = _adaln_all(cond, w_mod, b_mod).reshape(L, BP, 6, D)

    xa = jnp.concatenate([ctx, x], axis=1)
    column_order = False
    for layer in range(L):
        last = layer == L - 1
        want_columns = layer % 2 == 1
        if want_columns != column_order:
            xa = _grid_transpose_latent(xa, CTX, GRID_W if column_order else rows, rows if column_order else GRID_W)
            column_order = want_columns
        mod = mods[layer]

        wi = w_in[layer]
        o = 0
        wq = wi[:, o:o + MLW]; o += MLW
        wk = wi[:, o:o + MLW] * (ML_HD ** -0.5); o += MLW
        wv = wi[:, o:o + MLW]; o += MLW
        wo = wi[:, o:o + MLW]; o += MLW
        wif = wi[:, o:o + 4 * ML_HEADS]; o += 4 * ML_HEADS
        wlx = wi[:, o:o + D]; o += D
        wlg = wi[:, o:o + D]; o += D
        rw_w = 3 * D + 3 * RW_RANK
        wrw = wi[:, o:o + rw_w]; o += rw_w
        wmg = wi[:, o:o + 3 * D]; o += 3 * D
        w_qkv = jnp.concatenate([wq, wk, wv], axis=1).astype(BF16)
        w_if = jnp.concatenate([wif, jnp.zeros((D, 128 - 4 * ML_HEADS), F32)], axis=1).astype(BF16)

        h = _prenorm(xa, norm1_w[layer], mod, nc).reshape(N, D)
        qkv = _matmul(h, w_qkv, BF16, 1024).reshape(B, TA, 3 * MLW)
        ml_o = _matmul(h, wo.astype(BF16), F32, 1024).reshape(B, TA, MLW)
        gcol = _matmul(h, w_if, F32, 128).reshape(B, TA, 128)
        lru_x = _matmul(h, wlx.astype(BF16), F32, 1024).reshape(B, TA, D)
        lru_g = _matmul(h, wlg.astype(BF16), F32, 1024).reshape(B, TA, D)
        rw = _matmul(h, wrw.astype(BF16), F32, rw_w // 3).reshape(B, TA, rw_w)
        mgates = _matmul(h, wmg.astype(BF16), F32, 1024).reshape(B, TA, 3 * D)

        grow = gcol[:, :, :16].reshape(B, TA // ML_CHUNK, ML_CHUNK, 16).transpose(0, 1, 3, 2)
        gate_b = jnp.concatenate([ml_ig_b[layer].reshape(-1), ml_fg_b[layer].reshape(-1)])
        bcol = jnp.concatenate([gate_b, jnp.zeros((128 - 16,), F32)]).reshape(1, 128)
        y_ml = _mlstm(qkv, gcol, grow, bcol, gate_b.reshape(16, 1), ml_o, ml_norm_w[layer], nc)

        y_lru = _lru(lru_x, lru_g, lru_conv_w[layer], lru_conv_b[layer],
                     _block_diag_groups(lru_gr_w[layer]), _block_diag_groups(lru_gi_w[layer]),
                     lru_gr_b[layer], lru_gi_b[layer], lru_lambda[layer], nc)

        y_rw = _rwkv(rw, rw_mu[layer], rw_k_k[layer], rw_k_a[layer], rw_decay0[layer],
                     _pad_dir_rows(rw_decay_up[layer]), rw_iclr0[layer], _pad_dir_rows(rw_iclr_up[layer]),
                     rw_gate_up[layer].astype(BF16), rw_r_k[layer], rw_ln_w[layer], rw_ln_b[layer], nc)

        xa, h2 = _merge(xa, y_ml, y_lru, y_rw, mgates, out_ml[layer].astype(BF16), out_lru[layer].astype(BF16),
                        out_rw[layer].astype(BF16), w_out[layer].astype(BF16), mod, norm2_w[layer], nc, last)
        xa = _ffn(xa, h2, w_ffn_in[layer].astype(BF16), w_ffn_out[layer].astype(BF16), mod, 0 if last else nc)

    out = _final_norm(xa, final_norm_w, 0)
    if column_order:
        out = out.reshape(B, GRID_W, rows, D).transpose(0, 2, 1, 3).reshape(B, T, D)
    return out
```

```python
import functools
import math

import jax
import jax.numpy as jnp
from jax import lax
from jax.experimental import pallas as pl
from jax.experimental.pallas import tpu as pltpu

F32 = jnp.float32
BF16 = jnp.bfloat16

NORM_EPS = 1e-6
GRID_W = 64
TOKEN_BLOCK = 256
HALO = 8

ML_HEADS = 4
ML_HD = 256
ML_CHUNK = 64
ML_M_INIT = -1e30

LRU_BLOCKS = 16
LRU_BD = 64
LRU_CONV = 4
LRU_C = 8.0
LRU_GROUP = 256

RW_HD = 64
RW_HEADS = 16
RW_PAIRS = 8
RW_GROUP = 8
RW_CHUNK = 128
RW_RANK = 128
RW_DECAY_SCALE = math.exp(-0.5)
RW_LN_EPS = 64e-5

VMEM_LIMIT = 56 * 1024 * 1024


def _cparams(*sem):
    return pltpu.CompilerParams(dimension_semantics=sem, vmem_limit_bytes=VMEM_LIMIT)


def _dot(a, b):
    return jnp.dot(a, b, preferred_element_type=F32)


def _dot_nt(a, b):
    return lax.dot_general(a, b, (((1,), (1,)), ((), ())), preferred_element_type=F32)


def _dot_tn(a, b):
    return lax.dot_general(a, b, (((0,), (0,)), ((), ())), preferred_element_type=F32)


def _split3(x):
    h = x.astype(BF16)
    r = x - h.astype(F32)
    m = r.astype(BF16)
    l = (r - m.astype(F32)).astype(BF16)
    return h, m, l


def _tri_sum_left(tri, x):
    w = x.shape[1]
    h, m, l = _split3(x)
    y = _dot(tri, jnp.concatenate([h, m, l], axis=1))
    return (y[:, :w] + y[:, w:2 * w]) + y[:, 2 * w:]


def _tri_sum_right(x, tri):
    r = x.shape[0]
    h, m, l = _split3(x)
    y = _dot(jnp.concatenate([h, m, l], axis=0), tri)
    return (y[:r] + y[r:2 * r]) + y[2 * r:]


def _group_sum(x, ones_bd):
    n = x.shape[0]
    h = x.astype(BF16)
    l = (x - h.astype(F32)).astype(BF16)
    y = _dot(jnp.concatenate([h, l], axis=0), ones_bd)
    return y[:n] + y[n:]


def _bwd_block(j, nc, nb):
    return jnp.where(j < nc, nc - 1 - j, nb + nc - 1 - j)


def _block_fn(reverse, nc, nb):
    if reverse:
        return lambda j: _bwd_block(j, nc, nb)
    return lambda j: j


def _mod_kernel(c_ref, w_ref, b_ref, o_ref):
    c = c_ref[...]
    s = (c * jax.nn.sigmoid(c)).astype(BF16)
    o_ref[0] = _dot(s, w_ref[0].astype(BF16)) + b_ref[0]


def _adaln_all(cond, w_mod, b_mod):
    L, D, D6 = w_mod.shape
    BP = cond.shape[0]
    tn = 1536
    return pl.pallas_call(
        _mod_kernel,
        grid=(L, D6 // tn),
        in_specs=[
            pl.BlockSpec((BP, D), lambda l, n: (0, 0)),
            pl.BlockSpec((1, D, tn), lambda l, n: (l, 0, n)),
            pl.BlockSpec((1, 1, tn), lambda l, n: (l, 0, n)),
        ],
        out_specs=pl.BlockSpec((1, BP, tn), lambda l, n: (l, 0, n)),
        out_shape=jax.ShapeDtypeStruct((L, BP, D6), F32),
        compiler_params=_cparams("arbitrary", "arbitrary"),
    )(cond, w_mod, b_mod.reshape(L, 1, D6))


def _rms(x, w):
    return x * lax.rsqrt(jnp.mean(x * x, axis=-1, keepdims=True) + NORM_EPS) * w


def _prenorm_kernel(x_ref, w_ref, mod_ref, h_ref):
    m = mod_ref[0]
    y = _rms(x_ref[0], w_ref[...])
    h_ref[0] = (y * (1.0 + m[1:2]) + m[0:1]).astype(BF16)


def _prenorm(x, norm_w, mod, nc):
    B, TA, D = x.shape
    nb = TA // TOKEN_BLOCK
    return pl.pallas_call(
        _prenorm_kernel,
        grid=(B, nb),
        in_specs=[
            pl.BlockSpec((1, TOKEN_BLOCK, D), lambda b, j: (b, j, 0)),
            pl.BlockSpec((1, D), lambda b, j: (0, 0)),
            pl.BlockSpec((1, 6, D), lambda b, j: (jnp.where(j < nc, B, b), 0, 0)),
        ],
        out_specs=pl.BlockSpec((1, TOKEN_BLOCK, D), lambda b, j: (b, j, 0)),
        out_shape=jax.ShapeDtypeStruct((B, TA, D), BF16),
        compiler_params=_cparams("arbitrary", "arbitrary"),
    )(x, norm_w.reshape(1, D), mod)


def _final_norm_kernel(x_ref, w_ref, o_ref):
    o_ref[0] = _rms(x_ref[0], w_ref[...])


def _final_norm(x, w, nc):
    B, TA, D = x.shape
    nl = TA // TOKEN_BLOCK - nc
    return pl.pallas_call(
        _final_norm_kernel,
        grid=(B, nl),
        in_specs=[
            pl.BlockSpec((1, TOKEN_BLOCK, D), lambda b, j: (b, j + nc, 0)),
            pl.BlockSpec((1, D), lambda b, j: (0, 0)),
        ],
        out_specs=pl.BlockSpec((1, TOKEN_BLOCK, D), lambda b, j: (b, j, 0)),
        out_shape=jax.ShapeDtypeStruct((B, nl * TOKEN_BLOCK, D), F32),
        compiler_params=_cparams("arbitrary", "arbitrary"),
    )(x, w.reshape(1, D))


def _mm_kernel(h_ref, w_ref, o_ref):
    o_ref[...] = _dot(h_ref[...], w_ref[...]).astype(o_ref.dtype)


def _pick(n, cands):
    for c in cands:
        if n % c == 0:
            return c
    raise ValueError(f"no block size for {n}")


def _matmul(h, w, out_dtype, tn):
    N, K = h.shape
    M = w.shape[1]
    tm = _pick(N, (1024, 512, 256))
    return pl.pallas_call(
        _mm_kernel,
        grid=(M // tn, N // tm),
        in_specs=[
            pl.BlockSpec((tm, K), lambda n, m: (m, 0)),
            pl.BlockSpec((K, tn), lambda n, m: (0, n)),
        ],
        out_specs=pl.BlockSpec((tm, tn), lambda n, m: (m, n)),
        out_shape=jax.ShapeDtypeStruct((N, M), out_dtype),
        compiler_params=_cparams("arbitrary", "arbitrary"),
    )(h, w)


def _mlstm_kernel(reverse, *refs):
    if reverse:
        (q_ref, k_ref, v_ref, gc_ref, gr_ref, bc_ref, br_ref, yf_ref, o_ref, nw_ref,
         out_ref, ct_ref, n_ref, m_ref) = refs
    else:
        (q_ref, k_ref, v_ref, gc_ref, gr_ref, bc_ref, br_ref,
         out_ref, ct_ref, n_ref, m_ref) = refs
    C = ML_CHUNK
    j = pl.program_id(1)

    @pl.when(j == 0)
    def _():
        ct_ref[...] = jnp.zeros_like(ct_ref)
        n_ref[...] = jnp.zeros_like(n_ref)
        m_ref[...] = jnp.full_like(m_ref, ML_M_INIT)

    row = lax.broadcasted_iota(jnp.int32, (C, C), 0)
    col = lax.broadcasted_iota(jnp.int32, (C, C), 1)
    causal = (col >= row) if reverse else (col <= row)
    tri_l = causal.astype(BF16)
    tri_r = ((row >= col) if reverse else (row <= col)).astype(BF16)
    last = 0 if reverse else C - 1
    d = 1 if reverse else 0

    n_chunks = TOKEN_BLOCK // C
    order = range(n_chunks - 1, -1, -1) if reverse else range(n_chunks)
    for c in order:
        t0 = c * C
        gc = gc_ref[0, t0:t0 + C, :] + bc_ref[...]
        gr = gr_ref[0, c] + br_ref[...]
        fc = _tri_sum_left(tri_l, jax.nn.log_sigmoid(gc))
        fr = _tri_sum_right(jax.nn.log_sigmoid(gr), tri_r)
        for h in range(ML_HEADS):
            gi = d * ML_HEADS + h
            i_c = gc[:, gi:gi + 1]
            f_c = fc[:, 8 + gi:9 + gi]
            i_r = gr[gi:gi + 1, :]
            f_r = fr[8 + gi:9 + gi, :]
            f_end = f_c[last:last + 1, :]
            m_st = m_ref[h][0:1, 0:1]
            logw = jnp.where(causal, f_c - f_r + i_r, -jnp.inf)
            inter = f_c + m_st
            m_t = jnp.maximum(jnp.max(logw, axis=1, keepdims=True), inter)
            w_intra = jnp.exp(logw - m_t)
            w_inter = jnp.exp(inter - m_t)
            sl = slice(h * ML_HD, (h + 1) * ML_HD)
            q = q_ref[0, t0:t0 + C, sl]
            k = k_ref[0, t0:t0 + C, sl]
            v = v_ref[0, t0:t0 + C, sl]
            s = _dot_nt(q, k) * w_intra
            ct = ct_ref[h]
            nv = n_ref[h][0:1, :]
            num = _dot(s.astype(BF16), v) + w_inter * _dot(q, ct.astype(BF16))
            den = jnp.sum(s, axis=1, keepdims=True) + w_inter * jnp.sum(q.astype(F32) * nv, axis=1, keepdims=True)
            h_out = num / jnp.maximum(jnp.abs(den), jnp.exp(-m_t))
            log_end_c = f_end - f_c + i_c
            log_end_r = f_end - f_r + i_r
            m_new = jnp.maximum(f_end + m_st, jnp.max(log_end_r, axis=1, keepdims=True))
            w_end = jnp.exp(log_end_c - m_new)
            decay = jnp.exp(f_end + m_st - m_new)
            wv = (w_end * v.astype(F32)).astype(BF16)
            ct_ref[h] = decay * ct + _dot_tn(k, wv)
            n_new = decay * nv + jnp.sum(w_end * k.astype(F32), axis=0, keepdims=True)
            n_ref[h] = jnp.broadcast_to(n_new, n_ref.shape[1:])
            m_ref[h] = jnp.broadcast_to(m_new, m_ref.shape[1:])
            if reverse:
                tot = yf_ref[0, t0:t0 + C, sl] + h_out
                hn = tot * lax.rsqrt(jnp.mean(tot * tot, axis=-1, keepdims=True) + NORM_EPS) * nw_ref[:, sl]
                out_ref[0, t0:t0 + C, sl] = (hn * jax.nn.sigmoid(o_ref[0, t0:t0 + C, sl])).astype(out_ref.dtype)
            else:
                out_ref[0, t0:t0 + C, sl] = h_out


def _mlstm(qkv, gcol, grow, bcol, brow, o, norm_w, nc):
    B, TA, W3 = qkv.shape
    W = W3 // 3
    nb = TA // TOKEN_BLOCK
    cpb = TOKEN_BLOCK // ML_CHUNK
    scratch = [
        pltpu.VMEM((ML_HEADS, ML_HD, ML_HD), F32),
        pltpu.VMEM((ML_HEADS, 8, ML_HD), F32),
        pltpu.VMEM((ML_HEADS, 8, 128), F32),
    ]

    def run(reverse, extra_in, extra_specs, out_dtype):
        blk = _block_fn(reverse, nc, nb)
        tok = lambda w, cb: pl.BlockSpec((1, TOKEN_BLOCK, w), lambda b, j: (b, blk(j), cb))
        in_specs = [
            tok(W, 0), tok(W, 1), tok(W, 2),
            pl.BlockSpec((1, TOKEN_BLOCK, 128), lambda b, j: (b, blk(j), 0)),
            pl.BlockSpec((1, cpb, 16, ML_CHUNK), lambda b, j: (b, blk(j), 0, 0)),
            pl.BlockSpec((1, 128), lambda b, j: (0, 0)),
            pl.BlockSpec((16, 1), lambda b, j: (0, 0)),
        ] + extra_specs(tok)
        return pl.pallas_call(
            functools.partial(_mlstm_kernel, reverse),
            grid=(B, nb),
            in_specs=in_specs,
            out_specs=tok(W, 0),
            out_shape=jax.ShapeDtypeStruct((B, TA, W), out_dtype),
            scratch_shapes=scratch,
            compiler_params=_cparams("arbitrary", "arbitrary"),
        )(qkv, qkv, qkv, gcol, grow, bcol, brow, *extra_in)

    yf = run(False, (), lambda tok: [], F32)
    return run(True, (yf, o, norm_w.reshape(1, W)),
               lambda tok: [tok(W, 0), tok(W, 0), pl.BlockSpec((1, W), lambda b, j: (0, 0))], BF16)


def _halo_specs(width, blk, TA):
    per = TOKEN_BLOCK // HALO
    last = TA // HALO - 1
    prev = pl.BlockSpec((1, HALO, width), lambda b, j: (b, jnp.maximum(blk(j) * per - 1, 0), 0))
    nxt = pl.BlockSpec((1, HALO, width), lambda b, j: (b, jnp.minimum((blk(j) + 1) * per, last), 0))
    return prev, nxt


def _neg_expm1(z):
    t = jnp.tanh(0.5 * z)
    return -2.0 * t / (1.0 - t)


def _lru_kernel(reverse, nc, nb, *refs):
    if reverse:
        (x_ref, xp_ref, xn_ref, cw_ref, cb_ref, wr_ref, wi_ref, br_ref, bi_ref, lam_ref,
         hf_ref, gate_ref, out_ref, xpad_s, a_s, b_s, h_s) = refs
    else:
        (x_ref, xp_ref, xn_ref, cw_ref, cb_ref, wr_ref, wi_ref, br_ref, bi_ref, lam_ref,
         out_ref, xpad_s, a_s, b_s, h_s) = refs
    TB = TOKEN_BLOCK
    j = pl.program_id(1)
    blk = _bwd_block(j, nc, nb) if reverse else j

    @pl.when(j == 0)
    def _():
        h_s[...] = jnp.zeros_like(h_s)

    seg_start = (blk == 0) | (blk == nc)
    seg_end = (blk == nc - 1) | (blk == nb - 1)
    xpad_s[0:HALO, :] = jnp.where(seg_start, 0.0, xp_ref[0])
    xpad_s[HALO:HALO + TB, :] = x_ref[0]
    xpad_s[HALO + TB:, :] = jnp.where(seg_end, 0.0, xn_ref[0])
    left = LRU_CONV // 2
    u = cb_ref[...]
    for tap in range(LRU_CONV):
        off = HALO - left + tap
        u = u + cw_ref[tap:tap + 1, :] * xpad_s[off:off + TB, :]
    ub = u.astype(BF16)
    W = u.shape[1]
    r_parts, i_parts = [], []
    for g in range(W // LRU_GROUP):
        ug = ub[:, g * LRU_GROUP:(g + 1) * LRU_GROUP]
        r_parts.append(_dot(ug, wr_ref[g]))
        i_parts.append(_dot(ug, wi_ref[g]))
    r = jax.nn.sigmoid(jnp.concatenate(r_parts, axis=1) + br_ref[...])
    i = jax.nn.sigmoid(jnp.concatenate(i_parts, axis=1) + bi_ref[...])
    log_a = -LRU_C * r * jax.nn.softplus(-lam_ref[...])
    a_s[...] = jnp.exp(log_a)
    b_s[...] = jnp.sqrt(_neg_expm1(2.0 * log_a)) * (i * u)

    def step(t, h):
        idx = (TB - 1 - t) if reverse else t
        h = a_s[pl.ds(idx, 1), :] * h + b_s[pl.ds(idx, 1), :]
        b_s[pl.ds(idx, 1), :] = h
        return h

    h_s[...] = lax.fori_loop(0, TB, step, h_s[...], unroll=8)
    if reverse:
        out_ref[0] = ((hf_ref[0] + b_s[...]) * jax.nn.gelu(gate_ref[0])).astype(out_ref.dtype)
    else:
        out_ref[0] = b_s[...]


def _lru(x, gate, cw, cb, wr, wi, br, bi, lam, nc):
    B, TA, W = x.shape
    nb = TA // TOKEN_BLOCK
    ng = W // LRU_GROUP
    scratch = [
        pltpu.VMEM((TOKEN_BLOCK + 2 * HALO, W), F32),
        pltpu.VMEM((TOKEN_BLOCK, W), F32),
        pltpu.VMEM((TOKEN_BLOCK, W), F32),
        pltpu.VMEM((1, W), F32),
    ]

    def run(d, extra_in, out_dtype):
        reverse = d == 1
        blk = _block_fn(reverse, nc, nb)
        tok = pl.BlockSpec((1, TOKEN_BLOCK, W), lambda b, j: (b, blk(j), 0))
        prev, nxt = _halo_specs(W, blk, TA)
        const = lambda shape: pl.BlockSpec(shape, lambda b, j: (0,) * len(shape))
        in_specs = [tok, prev, nxt, const((LRU_CONV, W)), const((1, W)),
                    const((ng, LRU_GROUP, LRU_GROUP)), const((ng, LRU_GROUP, LRU_GROUP)),
                    const((1, W)), const((1, W)), const((1, W))] + [tok] * len(extra_in)
        return pl.pallas_call(
            functools.partial(_lru_kernel, reverse, nc, nb),
            grid=(B, nb),
            in_specs=in_specs,
            out_specs=tok,
            out_shape=jax.ShapeDtypeStruct((B, TA, W), out_dtype),
            scratch_shapes=scratch,
            compiler_params=_cparams("arbitrary", "arbitrary"),
        )(x, x, x, cw, cb.reshape(1, W), wr[d], wi[d], br[d:d + 1], bi[d:d + 1], lam[d:d + 1], *extra_in)

    hf = run(0, (), F32)
    return run(1, (hf, gate), BF16)


def _rwkv_kernel(reverse, nc, nb, *refs):
    if reverse:
        (x_ref, xp_ref, xn_ref, mu_ref, kk_ref, ka_ref, d0_ref, du_ref, a0_ref, au_ref,
         a0f_ref, auf_ref, gu_ref, rk_ref, lnw_ref, lnb_ref, yf_ref,
         out_ref, xpad_s, r_s, kap_s, v_s, kt_s, b_s, lw_s, y_s, bon_s, g_s, st_s) = refs
    else:
        (x_ref, xp_ref, xn_ref, mu_ref, kk_ref, ka_ref, d0_ref, du_ref, a0_ref, au_ref,
         out_ref, xpad_s, r_s, kap_s, v_s, kt_s, b_s, lw_s, st_s) = refs
    TB = TOKEN_BLOCK
    C = RW_CHUNK
    P = 2 * RW_HD
    D = RW_HEADS * RW_HD
    j = pl.program_id(1)
    blk = _bwd_block(j, nc, nb) if reverse else j

    @pl.when(j == 0)
    def _():
        st_s[...] = jnp.zeros_like(st_s)

    seg_start = (blk == 0) | (blk == nc)
    seg_end = (blk == nc - 1) | (blk == nb - 1)
    xpad_s[0:HALO, :] = jnp.where(seg_start, 0.0, xp_ref[0])
    xpad_s[HALO:HALO + TB, :] = x_ref[0]
    xpad_s[HALO + TB:, :] = jnp.where(seg_end, 0.0, xn_ref[0])
    x = x_ref[0]
    nbr = 0.5 * (xpad_s[HALO - 1:HALO - 1 + TB, :] + xpad_s[HALO + 1:HALO + 1 + TB, :])
    seg = x + mu_ref[...] * (nbr - x)
    r = seg[:, 0:D]
    k = seg[:, D:2 * D]
    v = seg[:, 2 * D:3 * D]
    dd = seg[:, 3 * D:3 * D + RW_RANK]
    ad = seg[:, 3 * D + RW_RANK:3 * D + 2 * RW_RANK]
    gd = seg[:, 3 * D + 2 * RW_RANK:3 * D + 3 * RW_RANK]

    lane = lax.broadcasted_iota(jnp.int32, (P, P), 1)
    sub = lax.broadcasted_iota(jnp.int32, (P, P), 0)
    same_head = (lane < RW_HD) == (sub < RW_HD)
    ones_bd = same_head.astype(BF16)

    logw = -RW_DECAY_SCALE * jax.nn.sigmoid(d0_ref[...] + _dot(jnp.tanh(dd).astype(BF16), du_ref[...]))
    ad_b = ad.astype(BF16)
    a = jax.nn.sigmoid(a0_ref[...] + _dot(ad_b, au_ref[...]))
    kk = k * kk_ref[...]
    kt = k * (1.0 + (a - 1.0) * ka_ref[...])
    if reverse:
        a_f = jax.nn.sigmoid(a0f_ref[...] + _dot(ad_b, auf_ref[...]))
        bon = r * (kt + k * (1.0 + (a_f - 1.0) * ka_ref[...])) * rk_ref[...]
        g_out = _dot(jax.nn.sigmoid(gd).astype(BF16), gu_ref[...])
    for p in range(RW_PAIRS):
        sl = slice(p * P, (p + 1) * P)
        if reverse:
            bon_s[p] = bon[:, sl]
            g_s[p] = g_out[:, sl]
        kkp = kk[:, sl]
        nrm = jnp.maximum(jnp.sqrt(_group_sum(kkp * kkp, ones_bd)), 1e-12)
        kap = kkp / nrm
        r_s[p] = r[:, sl]
        v_s[p] = v[:, sl]
        kap_s[p] = kap
        kt_s[p] = kt[:, sl]
        b_s[p] = kap * a[:, sl]
        lw_s[p] = logw[:, sl]

    row = lax.broadcasted_iota(jnp.int32, (C, C), 0)
    col = lax.broadcasted_iota(jnp.int32, (C, C), 1)
    incl = (col >= row) if reverse else (col <= row)
    strict = (col > row) if reverse else (col < row)
    tri = incl.astype(BF16)
    eye = (row == col).astype(F32)
    merge_masks = []
    for i in range(int(math.log2(C))):
        same_parent = jnp.right_shift(row, i + 1) == jnp.right_shift(col, i + 1)
        merge_masks.append(same_parent & (jnp.right_shift(row, i) != jnp.right_shift(col, i)))
    lane_lo = lax.broadcasted_iota(jnp.int32, (1, P), 1) < RW_HD
    head_masks = (lane_lo, jnp.logical_not(lane_lo))
    mid = C // 2
    n_chunks = TB // C
    order = range(n_chunks - 1, -1, -1) if reverse else range(n_chunks)

    def group_body(gi, carry):
        pairs = [gi * RW_GROUP + q for q in range(RW_GROUP)]
        sel = lambda x0, x1: jnp.where(lane_lo, x0, x1)
        items = [(q, c) for q in range(RW_GROUP) for c in range(n_chunks)]
        pre = {}
        for q, c in items:
            p = pairs[q]
            rows = slice(c * C, (c + 1) * C)
            lw = lw_s[p, rows, :]
            rr = r_s[p, rows, :]
            kap = kap_s[p, rows, :]
            ktc = kt_s[p, rows, :]
            bb = b_s[p, rows, :]
            L = _tri_sum_left(tri, lw)
            Lx = L - lw
            Lm = L[mid:mid + 1, :]
            Lc = jnp.sum(lw, axis=0, keepdims=True)
            e_neg = jnp.exp(Lm - L)
            e_end = jnp.exp(Lc - L)
            xs = jnp.concatenate([kap * jnp.exp(Lx - Lm), rr * jnp.exp(L - Lm)], axis=0)
            ys = jnp.concatenate([bb * e_neg, ktc * e_neg], axis=0).astype(BF16)
            d = dict(
                kap_st=(kap * jnp.exp(Lx)).astype(BF16),
                r_st=(rr * jnp.exp(L)).astype(BF16),
                bk_end=jnp.concatenate([(bb * e_end).astype(BF16), (ktc * e_end).astype(BF16)], axis=0),
                vb=v_s[p, rows, :].astype(BF16),
                decay=jnp.exp(Lc),
                heads=[],
            )
            for hm in head_masks:
                g = _dot_nt(jnp.where(hm, xs, 0.0).astype(BF16), ys)
                d["heads"].append(dict(
                    a=jnp.where(strict, g[:C, :C], 0.0),
                    bm_rk=jnp.concatenate([jnp.where(strict, g[:C, C:], 0.0),
                                           jnp.where(incl, g[C:, C:], 0.0)], axis=0).astype(BF16),
                    rb=jnp.where(incl, g[C:, :C], 0.0).astype(BF16),
                ))
            pre[(q, c)] = d

        chains = [pre[it]["heads"][h] for it in items for h in range(2)]
        for ch in chains:
            ch["t"] = eye - jnp.where(merge_masks[0], ch["a"], 0.0)
        for mm in merge_masks[1:]:
            for ch in chains:
                ch["tb"] = ch["t"].astype(BF16)
                ch["m1"] = _dot(jnp.where(mm, ch["a"], 0.0).astype(BF16), ch["tb"]).astype(BF16)
            for ch in chains:
                ch["t"] = ch["t"] - _dot(ch["tb"], ch["m1"])

        for it in items:
            d = pre[it]
            h0, h1 = d["heads"]
            x0 = _dot(h0["bm_rk"], d["vb"])
            x1 = _dot(h1["bm_rk"], d["vb"])
            bv = sel(x0[:C], x1[:C])
            d["rkv"] = sel(x0[C:], x1[C:])
            rhs = jnp.concatenate([d["kap_st"], bv.astype(BF16)], axis=1)
            tk0 = _dot(h0["t"].astype(BF16), rhs)
            tk1 = _dot(h1["t"].astype(BF16), rhs)
            d["w_st"] = sel(tk0[:, :P], tk1[:, :P]).astype(BF16)
            d["u_free"] = sel(tk0[:, P:], tk1[:, P:])

        for q in range(RW_GROUP):
            p = pairs[q]
            st = st_s[p]
            for c in order:
                d = pre[(q, c)]
                rows = slice(c * C, (c + 1) * C)
                ws_rs = _dot_nt(jnp.concatenate([d["w_st"], d["r_st"]], axis=0), st.astype(BF16))
                ub = (-(ws_rs[:C] + d["u_free"])).astype(BF16)
                y = ws_rs[C:] + d["rkv"] + sel(_dot(d["heads"][0]["rb"], ub), _dot(d["heads"][1]["rb"], ub))
                upd = _dot_tn(jnp.concatenate([ub, d["vb"]], axis=0), d["bk_end"])
                st = st * d["decay"] + jnp.where(same_head, upd, 0.0)
                if reverse:
                    y_s[p, rows, :] = y
                else:
                    out_ref[0, p, rows, :] = y
            st_s[p] = st
        return carry

    lax.fori_loop(0, RW_PAIRS // RW_GROUP, group_body, 0)

    if reverse:
        inv = 1.0 / RW_HD
        for p in range(RW_PAIRS):
            sl = slice(p * P, (p + 1) * P)
            y = yf_ref[0, p] + y_s[p]
            mu = _group_sum(y, ones_bd) * inv
            dlt = y - mu
            var = _group_sum(dlt * dlt, ones_bd) * inv
            yn = dlt * lax.rsqrt(var + RW_LN_EPS) * lnw_ref[:, sl] + lnb_ref[:, sl]
            bonus = _group_sum(bon_s[p], ones_bd) * v_s[p]
            out_ref[0, :, sl] = ((yn + bonus) * g_s[p]).astype(out_ref.dtype)


def _rwkv(x, mu, k_k, k_a, d0, du, a0, au, gate_up, r_k, ln_w, ln_b, nc):
    B, TA, WS = x.shape
    D = RW_HEADS * RW_HD
    P = 2 * RW_HD
    nb = TA // TOKEN_BLOCK
    slab = lambda: pltpu.VMEM((RW_PAIRS, TOKEN_BLOCK, P), F32)
    const = lambda shape: pl.BlockSpec(shape, lambda b, j: (0,) * len(shape))
    row = lambda a: a.reshape(1, -1)

    def run(d, extra_in, extra_specs, out_spec, out_shape, extra_scratch):
        reverse = d == 1
        blk = _block_fn(reverse, nc, nb)
        tok = pl.BlockSpec((1, TOKEN_BLOCK, WS), lambda b, j: (b, blk(j), 0))
        prev, nxt = _halo_specs(WS, blk, TA)
        in_specs = [tok, prev, nxt, const((1, WS)), const((1, D)), const((1, D)),
                    const((1, D)), const((RW_RANK, D)), const((1, D)), const((RW_RANK, D))] + extra_specs(blk)
        scratch = [pltpu.VMEM((TOKEN_BLOCK + 2 * HALO, WS), F32)] + [slab() for _ in range(6)] + extra_scratch + [
            pltpu.VMEM((RW_PAIRS, P, P), F32)]
        return pl.pallas_call(
            functools.partial(_rwkv_kernel, reverse, nc, nb),
            grid=(B, nb),
            in_specs=in_specs,
            out_specs=out_spec(blk),
            out_shape=out_shape,
            scratch_shapes=scratch,
            compiler_params=_cparams("arbitrary", "arbitrary"),
        )(x, x, x, row(mu), row(k_k), row(k_a), d0[d:d + 1], du[d], a0[d:d + 1], au[d], *extra_in)

    pair_spec = lambda blk: pl.BlockSpec((1, RW_PAIRS, TOKEN_BLOCK, P), lambda b, j: (b, 0, blk(j), 0))
    yf = run(0, (), lambda blk: [], pair_spec, jax.ShapeDtypeStruct((B, RW_PAIRS, TA, P), F32), [])
    return run(
        1,
        (a0[0:1], au[0], gate_up, row(r_k), row(ln_w), row(ln_b), yf),
        lambda blk: [const((1, D)), const((RW_RANK, D)), const((RW_RANK, D)), const((1, D)), const((1, D)),
                     const((1, D)), pair_spec(blk)],
        lambda blk: pl.BlockSpec((1, TOKEN_BLOCK, D), lambda b, j: (b, blk(j), 0)),
        jax.ShapeDtypeStruct((B, TA, D), BF16),
        [slab(), slab(), slab()],
    )


def _merge_kernel(x_ref, yml_ref, ylru_ref, yrw_ref, g_ref, oml_ref, olru_ref, orw_ref, wout_ref,
                  mod_ref, nw_ref, xo_ref, h_ref):
    D = x_ref.shape[-1]
    g = g_ref[0]
    acc = jax.nn.sigmoid(g[:, 0:D]) * _dot(yml_ref[0], oml_ref[...])
    acc = acc + jax.nn.sigmoid(g[:, D:2 * D]) * _dot(ylru_ref[0], olru_ref[...])
    acc = acc + jax.nn.sigmoid(g[:, 2 * D:3 * D]) * _dot(yrw_ref[0], orw_ref[...])
    y = _dot(acc.astype(BF16), wout_ref[...])
    m = mod_ref[0]
    xn = x_ref[0] + m[2:3] * y
    xo_ref[0] = xn
    h_ref[0] = (_rms(xn, nw_ref[...]) * (1.0 + m[4:5]) + m[3:4]).astype(BF16)


def _merge(x, yml, ylru, yrw, gates, oml, olru, orw, wout, mod, norm2_w, nc, skip_ctx):
    B, TA, D = x.shape
    nb = TA // TOKEN_BLOCK
    off = nc if skip_ctx else 0
    rows_out = (nb - off) * TOKEN_BLOCK
    tok = lambda w: pl.BlockSpec((1, TOKEN_BLOCK, w), lambda b, j: (b, j + off, 0))
    otok = pl.BlockSpec((1, TOKEN_BLOCK, D), lambda b, j: (b, j, 0))
    wspec = pl.BlockSpec((D, D), lambda b, j: (0, 0))
    return pl.pallas_call(
        _merge_kernel,
        grid=(B, nb - off),
        in_specs=[tok(D), tok(D), tok(D), tok(D), tok(3 * D), wspec, wspec, wspec, wspec,
                  pl.BlockSpec((1, 6, D), lambda b, j: (jnp.where(j + off < nc, B, b), 0, 0)),
                  pl.BlockSpec((1, D), lambda b, j: (0, 0))],
        out_specs=[otok, otok],
        out_shape=[jax.ShapeDtypeStruct((B, rows_out, D), F32), jax.ShapeDtypeStruct((B, rows_out, D), BF16)],
        input_output_aliases={} if skip_ctx else {0: 0},
        compiler_params=_cparams("arbitrary", "arbitrary"),
    )(x, yml, ylru, yrw, gates, oml, olru, orw, wout, mod, norm2_w.reshape(1, D))


def _ffn_kernel(x_ref, h_ref, win_ref, wout_ref, mod_ref, xo_ref):
    F = wout_ref.shape[0]
    u = _dot(h_ref[0], win_ref[...])
    gate = u[:, :F]
    act = (gate * jax.nn.sigmoid(gate) * u[:, F:]).astype(BF16)
    xo_ref[0] = x_ref[0] + mod_ref[0][5:6] * _dot(act, wout_ref[...])


def _ffn(x, h, win, wout, mod, nc):
    B, TA, D = x.shape
    F = wout.shape[0]
    nb = TA // TOKEN_BLOCK
    tok = pl.BlockSpec((1, TOKEN_BLOCK, D), lambda b, j: (b, j, 0))
    return pl.pallas_call(
        _ffn_kernel,
        grid=(B, nb),
        in_specs=[tok, tok,
                  pl.BlockSpec((D, 2 * F), lambda b, j: (0, 0)),
                  pl.BlockSpec((F, D), lambda b, j: (0, 0)),
                  pl.BlockSpec((1, 6, D), lambda b, j: (jnp.where(j < nc, B, b), 0, 0))],
        out_specs=tok,
        out_shape=jax.ShapeDtypeStruct((B, TA, D), F32),
        input_output_aliases={0: 0},
        compiler_params=_cparams("arbitrary", "arbitrary"),
    )(x, h, win, wout, mod)


def _grid_transpose_latent(xa, ctx_len, rows, cols):
    B, TA, D = xa.shape
    lat = xa[:, ctx_len:].reshape(B, rows, cols, D).transpose(0, 2, 1, 3).reshape(B, rows * cols, D)
    return jnp.concatenate([xa[:, :ctx_len], lat], axis=1)


def _block_diag_groups(w):
    per = LRU_GROUP // LRU_BD
    d2 = w.shape[0]
    wg = w.reshape(d2, LRU_BLOCKS // per, per, LRU_BD, LRU_BD)
    eye = jnp.eye(per, dtype=w.dtype)
    out = jnp.einsum('dgpio,pq->dgpiqo', wg, eye)
    return out.reshape(d2, LRU_BLOCKS // per, LRU_GROUP, LRU_GROUP).astype(BF16)


def _pad_dir_rows(w):
    z = jnp.zeros_like(w[0])
    return jnp.stack([jnp.concatenate([w[0], z], axis=0), jnp.concatenate([z, w[1]], axis=0)]).astype(BF16)


def kernel(x, c, ctx, c_ctx, w_mod, b_mod, norm1_w, norm2_w, w_in, ml_ig_b, ml_fg_b, ml_norm_w, lru_conv_w, lru_conv_b, lru_gr_w, lru_gr_b, lru_gi_w, lru_gi_b, lru_lambda, rw_mu, rw_decay0, rw_decay_up, rw_iclr0, rw_iclr_up, rw_gate_up, rw_k_k, rw_k_a, rw_r_k, rw_ln_w, rw_ln_b, out_ml, out_lru, out_rw, w_out, w_ffn_in, w_ffn_out, final_norm_w):
    B, T, D = x.shape
    CTX = ctx.shape[1]
    L = w_mod.shape[0]
    TA = CTX + T
    assert CTX % TOKEN_BLOCK == 0 and T % TOKEN_BLOCK == 0 and T % GRID_W == 0
    nc = CTX // TOKEN_BLOCK
    rows = T // GRID_W
    N = B * TA
    MLW = ML_HEADS * ML_HD

    BP = -(-(B + 1) // 8) * 8
    cond = jnp.concatenate([c, c_ctx[None, :], jnp.zeros((BP - B - 1, D), F32)], axis=0)
    mods = _adaln_all(cond, w_mod, b_mod).reshape(L, BP, 6, D)

    xa = jnp.concatenate([ctx, x], axis=1)
    column_order = False
    for layer in range(L):
        last = layer == L - 1
        want_columns = layer % 2 == 1
        if want_columns != column_order:
            xa = _grid_transpose_latent(xa, CTX, GRID_W if column_order else rows, rows if column_order else GRID_W)
            column_order = want_columns
        mod = mods[layer]

        wi = w_in[layer]
        o = 0
        wq = wi[:, o:o + MLW]; o += MLW
        wk = wi[:, o:o + MLW] * (ML_HD ** -0.5); o += MLW
        wv = wi[:, o:o + MLW]; o += MLW
        wo = wi[:, o:o + MLW]; o += MLW
        wif = wi[:, o:o + 4 * ML_HEADS]; o += 4 * ML_HEADS
        wlx = wi[:, o:o + D]; o += D
        wlg = wi[:, o:o + D]; o += D
        rw_w = 3 * D + 3 * RW_RANK
        wrw = wi[:, o:o + rw_w]; o += rw_w
        wmg = wi[:, o:o + 3 * D]; o += 3 * D
        w_qkv = jnp.concatenate([wq, wk, wv], axis=1).astype(BF16)
        w_if = jnp.concatenate([wif, jnp.zeros((D, 128 - 4 * ML_HEADS), F32)], axis=1).astype(BF16)

        h = _prenorm(xa, norm1_w[layer], mod, nc).reshape(N, D)
        qkv = _matmul(h, w_qkv, BF16, 1024).reshape(B, TA, 3 * MLW)
        ml_o = _matmul(h, wo.astype(BF16), F32, 1024).reshape(B, TA, MLW)
        gcol = _matmul(h, w_if, F32, 128).reshape(B, TA, 128)
        lru_x = _matmul(h, wlx.astype(BF16), F32, 1024).reshape(B, TA, D)
        lru_g = _matmul(h, wlg.astype(BF16), F32, 1024).reshape(B, TA, D)
        rw = _matmul(h, wrw.astype(BF16), F32, rw_w // 3).reshape(B, TA, rw_w)
        mgates = _matmul(h, wmg.astype(BF16), F32, 1024).reshape(B, TA, 3 * D)

        grow = gcol[:, :, :16].reshape(B, TA // ML_CHUNK, ML_CHUNK, 16).transpose(0, 1, 3, 2)
        gate_b = jnp.concatenate([ml_ig_b[layer].reshape(-1), ml_fg_b[layer].reshape(-1)])
        bcol = jnp.concatenate([gate_b, jnp.zeros((128 - 16,), F32)]).reshape(1, 128)
        y_ml = _mlstm(qkv, gcol, grow, bcol, gate_b.reshape(16, 1), ml_o, ml_norm_w[layer], nc)

        y_lru = _lru(lru_x, lru_g, lru_conv_w[layer], lru_conv_b[layer],
                     _block_diag_groups(lru_gr_w[layer]), _block_diag_groups(lru_gi_w[layer]),
                     lru_gr_b[layer], lru_gi_b[layer], lru_lambda[layer], nc)

        y_rw = _rwkv(rw, rw_mu[layer], rw_k_k[layer], rw_k_a[layer], rw_decay0[layer],
                     _pad_dir_rows(rw_decay_up[layer]), rw_iclr0[layer], _pad_dir_rows(rw_iclr_up[layer]),
                     rw_gate_up[layer].astype(BF16), rw_r_k[layer], rw_ln_w[layer], rw_ln_b[layer], nc)

        xa, h2 = _merge(xa, y_ml, y_lru, y_rw, mgates, out_ml[layer].astype(BF16), out_lru[layer].astype(BF16),
                        out_rw[layer].astype(BF16), w_out[layer].astype(BF16), mod, norm2_w[layer], nc, last)
        xa = _ffn(xa, h2, w_ffn_in[layer].astype(BF16), w_ffn_out[layer].astype(BF16), mod, 0 if last else nc)

    out = _final_norm(xa, final_norm_w, 0)
    if column_order:
        out = out.reshape(B, GRID_W, rows, D).transpose(0, 2, 1, 3).reshape(B, T, D)
    return out
```

```python
import functools
import math

import jax
import jax.numpy as jnp
from jax import lax
from jax.experimental import pallas as pl
from jax.experimental.pallas import tpu as pltpu

F32 = jnp.float32
BF16 = jnp.bfloat16

NORM_EPS = 1e-6
GRID_W = 64
TOKEN_BLOCK = 256
HALO = 8

ML_HEADS = 4
ML_HD = 256
ML_CHUNK = 64
ML_M_INIT = -1e30

LRU_BLOCKS = 16
LRU_BD = 64
LRU_CONV = 4
LRU_C = 8.0
LRU_GROUP = 256

RW_HD = 64
RW_HEADS = 16
RW_PAIRS = 8
RW_GROUP = 8
RW_CHUNK = 128
RW_RANK = 128
RW_DECAY_SCALE = math.exp(-0.5)
RW_LN_EPS = 64e-5

VMEM_LIMIT = 56 * 1024 * 1024


def _cparams(*sem):
    return pltpu.CompilerParams(dimension_semantics=sem, vmem_limit_bytes=VMEM_LIMIT)


def _dot(a, b):
    return jnp.dot(a, b, preferred_element_type=F32)


def _dot_nt(a, b):
    return lax.dot_general(a, b, (((1,), (1,)), ((), ())), preferred_element_type=F32)


def _dot_tn(a, b):
    return lax.dot_general(a, b, (((0,), (0,)), ((), ())), preferred_element_type=F32)


def _split2(x):
    h = x.astype(BF16)
    return h, (x - h.astype(F32)).astype(BF16)


def _tri_sum_left(tri, x):
    w = x.shape[1]
    y = _dot(tri, jnp.concatenate(_split2(x), axis=1))
    return y[:, :w] + y[:, w:]


def _tri_sum_right(x, tri):
    r = x.shape[0]
    y = _dot(jnp.concatenate(_split2(x), axis=0), tri)
    return y[:r] + y[r:]


def _group_sum(x, ones_bd):
    n = x.shape[0]
    h = x.astype(BF16)
    l = (x - h.astype(F32)).astype(BF16)
    y = _dot(jnp.concatenate([h, l], axis=0), ones_bd)
    return y[:n] + y[n:]


def _block_diag2(x, left):
    zero = jnp.zeros_like(x)
    return jnp.concatenate([jnp.where(left, x, zero), jnp.where(left, zero, x)], axis=0)


def _head_stack(z, head0):
    zero = jnp.zeros_like(z)
    return jnp.concatenate([jnp.where(head0, z, zero), jnp.where(head0, zero, z)], axis=0)


def _bwd_block(j, nc, nb):
    return jnp.where(j < nc, nc - 1 - j, nb + nc - 1 - j)


def _block_fn(reverse, nc, nb):
    if reverse:
        return lambda j: _bwd_block(j, nc, nb)
    return lambda j: j


def _mod_kernel(c_ref, w_ref, b_ref, o_ref):
    c = c_ref[...]
    s = (c * jax.nn.sigmoid(c)).astype(BF16)
    o_ref[0] = _dot(s, w_ref[0].astype(BF16)) + b_ref[0]


def _adaln_all(cond, w_mod, b_mod):
    L, D, D6 = w_mod.shape
    BP = cond.shape[0]
    tn = 1536
    return pl.pallas_call(
        _mod_kernel,
        grid=(L, D6 // tn),
        in_specs=[
            pl.BlockSpec((BP, D), lambda l, n: (0, 0)),
            pl.BlockSpec((1, D, tn), lambda l, n: (l, 0, n)),
            pl.BlockSpec((1, 1, tn), lambda l, n: (l, 0, n)),
        ],
        out_specs=pl.BlockSpec((1, BP, tn), lambda l, n: (l, 0, n)),
        out_shape=jax.ShapeDtypeStruct((L, BP, D6), F32),
        compiler_params=_cparams("arbitrary", "arbitrary"),
    )(cond, w_mod, b_mod.reshape(L, 1, D6))


def _rms(x, w):
    return x * lax.rsqrt(jnp.mean(x * x, axis=-1, keepdims=True) + NORM_EPS) * w


def _prenorm_kernel(x_ref, w_ref, mod_ref, h_ref):
    m = mod_ref[0]
    y = _rms(x_ref[0], w_ref[...])
    h_ref[0] = (y * (1.0 + m[1:2]) + m[0:1]).astype(BF16)


def _prenorm(x, norm_w, mod, nc):
    B, TA, D = x.shape
    nb = TA // TOKEN_BLOCK
    return pl.pallas_call(
        _prenorm_kernel,
        grid=(B, nb),
        in_specs=[
            pl.BlockSpec((1, TOKEN_BLOCK, D), lambda b, j: (b, j, 0)),
            pl.BlockSpec((1, D), lambda b, j: (0, 0)),
            pl.BlockSpec((1, 6, D), lambda b, j: (jnp.where(j < nc, B, b), 0, 0)),
        ],
        out_specs=pl.BlockSpec((1, TOKEN_BLOCK, D), lambda b, j: (b, j, 0)),
        out_shape=jax.ShapeDtypeStruct((B, TA, D), BF16),
        compiler_params=_cparams("arbitrary", "arbitrary"),
    )(x, norm_w.reshape(1, D), mod)


def _final_norm_kernel(x_ref, w_ref, o_ref):
    o_ref[0] = _rms(x_ref[0], w_ref[...])


def _final_norm(x, w, nc):
    B, TA, D = x.shape
    nl = TA // TOKEN_BLOCK - nc
    return pl.pallas_call(
        _final_norm_kernel,
        grid=(B, nl),
        in_specs=[
            pl.BlockSpec((1, TOKEN_BLOCK, D), lambda b, j: (b, j + nc, 0)),
            pl.BlockSpec((1, D), lambda b, j: (0, 0)),
        ],
        out_specs=pl.BlockSpec((1, TOKEN_BLOCK, D), lambda b, j: (b, j, 0)),
        out_shape=jax.ShapeDtypeStruct((B, nl * TOKEN_BLOCK, D), F32),
        compiler_params=_cparams("arbitrary", "arbitrary"),
    )(x, w.reshape(1, D))


def _mm_kernel(h_ref, w_ref, o_ref):
    o_ref[...] = _dot(h_ref[...], w_ref[...]).astype(o_ref.dtype)


def _pick(n, cands):
    for c in cands:
        if n % c == 0:
            return c
    raise ValueError(f"no block size for {n}")


def _matmul(h, w, out_dtype, tn):
    N, K = h.shape
    M = w.shape[1]
    tm = _pick(N, (1024, 512, 256))
    return pl.pallas_call(
        _mm_kernel,
        grid=(M // tn, N // tm),
        in_specs=[
            pl.BlockSpec((tm, K), lambda n, m: (m, 0)),
            pl.BlockSpec((K, tn), lambda n, m: (0, n)),
        ],
        out_specs=pl.BlockSpec((tm, tn), lambda n, m: (m, n)),
        out_shape=jax.ShapeDtypeStruct((N, M), out_dtype),
        compiler_params=_cparams("arbitrary", "arbitrary"),
    )(h, w)


def _mlstm_kernel(reverse, *refs):
    if reverse:
        (q_ref, k_ref, v_ref, gc_ref, gr_ref, bc_ref, br_ref, yf_ref, o_ref, nw_ref,
         out_ref, ct_ref, n_ref, m_ref) = refs
    else:
        (q_ref, k_ref, v_ref, gc_ref, gr_ref, bc_ref, br_ref,
         out_ref, ct_ref, n_ref, m_ref) = refs
    C = ML_CHUNK
    j = pl.program_id(1)

    @pl.when(j == 0)
    def _():
        ct_ref[...] = jnp.zeros_like(ct_ref)
        n_ref[...] = jnp.zeros_like(n_ref)
        m_ref[...] = jnp.full_like(m_ref, ML_M_INIT)

    row = lax.broadcasted_iota(jnp.int32, (C, C), 0)
    col = lax.broadcasted_iota(jnp.int32, (C, C), 1)
    causal = (col >= row) if reverse else (col <= row)
    tri_l = causal.astype(BF16)
    tri_r = ((row >= col) if reverse else (row <= col)).astype(BF16)
    last = 0 if reverse else C - 1
    d = 1 if reverse else 0

    n_chunks = TOKEN_BLOCK // C
    order = range(n_chunks - 1, -1, -1) if reverse else range(n_chunks)
    for c in order:
        t0 = c * C
        gc = gc_ref[0, t0:t0 + C, :] + bc_ref[...]
        gr = gr_ref[0, c] + br_ref[...]
        fc = _tri_sum_left(tri_l, jax.nn.log_sigmoid(gc))
        fr = _tri_sum_right(jax.nn.log_sigmoid(gr), tri_r)
        for h in range(ML_HEADS):
            gi = d * ML_HEADS + h
            i_c = gc[:, gi:gi + 1]
            f_c = fc[:, 8 + gi:9 + gi]
            i_r = gr[gi:gi + 1, :]
            f_r = fr[8 + gi:9 + gi, :]
            f_end = f_c[last:last + 1, :]
            m_st = m_ref[h][0:1, 0:1]
            logw = jnp.where(causal, f_c - f_r + i_r, -jnp.inf)
            inter = f_c + m_st
            m_t = jnp.maximum(jnp.max(logw, axis=1, keepdims=True), inter)
            w_intra = jnp.exp(logw - m_t)
            w_inter = jnp.exp(inter - m_t)
            sl = slice(h * ML_HD, (h + 1) * ML_HD)
            q = q_ref[0, t0:t0 + C, sl]
            k = k_ref[0, t0:t0 + C, sl]
            v = v_ref[0, t0:t0 + C, sl]
            s = _dot_nt(q, k) * w_intra
            ct = ct_ref[h]
            nv = n_ref[h][0:1, :]
            num = _dot(s.astype(BF16), v) + w_inter * _dot(q, ct.astype(BF16))
            den = jnp.sum(s, axis=1, keepdims=True) + w_inter * jnp.sum(q.astype(F32) * nv, axis=1, keepdims=True)
            h_out = num / jnp.maximum(jnp.abs(den), jnp.exp(-m_t))
            log_end_c = f_end - f_c + i_c
            log_end_r = f_end - f_r + i_r
            m_new = jnp.maximum(f_end + m_st, jnp.max(log_end_r, axis=1, keepdims=True))
            w_end = jnp.exp(log_end_c - m_new)
            decay = jnp.exp(f_end + m_st - m_new)
            wv = (w_end * v.astype(F32)).astype(BF16)
            ct_ref[h] = decay * ct + _dot_tn(k, wv)
            n_new = decay * nv + jnp.sum(w_end * k.astype(F32), axis=0, keepdims=True)
            n_ref[h] = jnp.broadcast_to(n_new, n_ref.shape[1:])
            m_ref[h] = jnp.broadcast_to(m_new, m_ref.shape[1:])
            if reverse:
                tot = yf_ref[0, t0:t0 + C, sl] + h_out
                hn = tot * lax.rsqrt(jnp.mean(tot * tot, axis=-1, keepdims=True) + NORM_EPS) * nw_ref[:, sl]
                out_ref[0, t0:t0 + C, sl] = (hn * jax.nn.sigmoid(o_ref[0, t0:t0 + C, sl])).astype(out_ref.dtype)
            else:
                out_ref[0, t0:t0 + C, sl] = h_out


def _mlstm(qkv, gcol, grow, bcol, brow, o, norm_w, nc):
    B, TA, W3 = qkv.shape
    W = W3 // 3
    nb = TA // TOKEN_BLOCK
    cpb = TOKEN_BLOCK // ML_CHUNK
    scratch = [
        pltpu.VMEM((ML_HEADS, ML_HD, ML_HD), F32),
        pltpu.VMEM((ML_HEADS, 8, ML_HD), F32),
        pltpu.VMEM((ML_HEADS, 8, 128), F32),
    ]

    def run(reverse, extra_in, extra_specs, out_dtype):
        blk = _block_fn(reverse, nc, nb)
        tok = lambda w, cb: pl.BlockSpec((1, TOKEN_BLOCK, w), lambda b, j: (b, blk(j), cb))
        in_specs = [
            tok(W, 0), tok(W, 1), tok(W, 2),
            pl.BlockSpec((1, TOKEN_BLOCK, 128), lambda b, j: (b, blk(j), 0)),
            pl.BlockSpec((1, cpb, 16, ML_CHUNK), lambda b, j: (b, blk(j), 0, 0)),
            pl.BlockSpec((1, 128), lambda b, j: (0, 0)),
            pl.BlockSpec((16, 1), lambda b, j: (0, 0)),
        ] + extra_specs(tok)
        return pl.pallas_call(
            functools.partial(_mlstm_kernel, reverse),
            grid=(B, nb),
            in_specs=in_specs,
            out_specs=tok(W, 0),
            out_shape=jax.ShapeDtypeStruct((B, TA, W), out_dtype),
            scratch_shapes=scratch,
            compiler_params=_cparams("arbitrary", "arbitrary"),
        )(qkv, qkv, qkv, gcol, grow, bcol, brow, *extra_in)

    yf = run(False, (), lambda tok: [], F32)
    return run(True, (yf, o, norm_w.reshape(1, W)),
               lambda tok: [tok(W, 0), tok(W, 0), pl.BlockSpec((1, W), lambda b, j: (0, 0))], BF16)


def _halo_specs(width, blk, TA):
    per = TOKEN_BLOCK // HALO
    last = TA // HALO - 1
    prev = pl.BlockSpec((1, HALO, width), lambda b, j: (b, jnp.maximum(blk(j) * per - 1, 0), 0))
    nxt = pl.BlockSpec((1, HALO, width), lambda b, j: (b, jnp.minimum((blk(j) + 1) * per, last), 0))
    return prev, nxt


def _neg_expm1(z):
    t = jnp.tanh(0.5 * z)
    return -2.0 * t / (1.0 - t)


def _lru_kernel(reverse, nc, nb, *refs):
    if reverse:
        (x_ref, xp_ref, xn_ref, cw_ref, cb_ref, wr_ref, wi_ref, br_ref, bi_ref, lam_ref,
         hf_ref, gate_ref, out_ref, xpad_s, a_s, b_s, h_s) = refs
    else:
        (x_ref, xp_ref, xn_ref, cw_ref, cb_ref, wr_ref, wi_ref, br_ref, bi_ref, lam_ref,
         out_ref, xpad_s, a_s, b_s, h_s) = refs
    TB = TOKEN_BLOCK
    j = pl.program_id(1)
    blk = _bwd_block(j, nc, nb) if reverse else j

    @pl.when(j == 0)
    def _():
        h_s[...] = jnp.zeros_like(h_s)

    seg_start = (blk == 0) | (blk == nc)
    seg_end = (blk == nc - 1) | (blk == nb - 1)
    xpad_s[0:HALO, :] = jnp.where(seg_start, 0.0, xp_ref[0])
    xpad_s[HALO:HALO + TB, :] = x_ref[0]
    xpad_s[HALO + TB:, :] = jnp.where(seg_end, 0.0, xn_ref[0])
    left = LRU_CONV // 2
    u = cb_ref[...]
    for tap in range(LRU_CONV):
        off = HALO - left + tap
        u = u + cw_ref[tap:tap + 1, :] * xpad_s[off:off + TB, :]
    ub = u.astype(BF16)
    W = u.shape[1]
    r_parts, i_parts = [], []
    for g in range(W // LRU_GROUP):
        ug = ub[:, g * LRU_GROUP:(g + 1) * LRU_GROUP]
        r_parts.append(_dot(ug, wr_ref[g]))
        i_parts.append(_dot(ug, wi_ref[g]))
    r = jax.nn.sigmoid(jnp.concatenate(r_parts, axis=1) + br_ref[...])
    i = jax.nn.sigmoid(jnp.concatenate(i_parts, axis=1) + bi_ref[...])
    log_a = -LRU_C * r * jax.nn.softplus(-lam_ref[...])
    a = jnp.exp(log_a)
    a_s[...] = a
    b_s[...] = jnp.sqrt(jnp.maximum(1.0 - a * a, 0.0)) * (i * u)

    def step(t, h):
        idx = (TB - 1 - t) if reverse else t
        h = a_s[pl.ds(idx, 1), :] * h + b_s[pl.ds(idx, 1), :]
        b_s[pl.ds(idx, 1), :] = h
        return h

    h_s[...] = lax.fori_loop(0, TB, step, h_s[...], unroll=8)
    if reverse:
        out_ref[0] = ((hf_ref[0] + b_s[...]) * jax.nn.gelu(gate_ref[0])).astype(out_ref.dtype)
    else:
        out_ref[0] = b_s[...]


def _lru(x, gate, cw, cb, wr, wi, br, bi, lam, nc):
    B, TA, W = x.shape
    nb = TA // TOKEN_BLOCK
    ng = W // LRU_GROUP
    scratch = [
        pltpu.VMEM((TOKEN_BLOCK + 2 * HALO, W), F32),
        pltpu.VMEM((TOKEN_BLOCK, W), F32),
        pltpu.VMEM((TOKEN_BLOCK, W), F32),
        pltpu.VMEM((1, W), F32),
    ]

    def run(d, extra_in, out_dtype):
        reverse = d == 1
        blk = _block_fn(reverse, nc, nb)
        tok = pl.BlockSpec((1, TOKEN_BLOCK, W), lambda b, j: (b, blk(j), 0))
        prev, nxt = _halo_specs(W, blk, TA)
        const = lambda shape: pl.BlockSpec(shape, lambda b, j: (0,) * len(shape))
        in_specs = [tok, prev, nxt, const((LRU_CONV, W)), const((1, W)),
                    const((ng, LRU_GROUP, LRU_GROUP)), const((ng, LRU_GROUP, LRU_GROUP)),
                    const((1, W)), const((1, W)), const((1, W))] + [tok] * len(extra_in)
        return pl.pallas_call(
            functools.partial(_lru_kernel, reverse, nc, nb),
            grid=(B, nb),
            in_specs=in_specs,
            out_specs=tok,
            out_shape=jax.ShapeDtypeStruct((B, TA, W), out_dtype),
            scratch_shapes=scratch,
            compiler_params=_cparams("arbitrary", "arbitrary"),
        )(x, x, x, cw, cb.reshape(1, W), wr[d], wi[d], br[d:d + 1], bi[d:d + 1], lam[d:d + 1], *extra_in)

    hf = run(0, (), F32)
    return run(1, (hf, gate), BF16)


def _rwkv_kernel_serial(reverse, nc, nb, *refs):
    if reverse:
        (x_ref, xp_ref, xn_ref, mu_ref, kk_ref, ka_ref, d0_ref, du_ref, a0_ref, au_ref,
         a0f_ref, auf_ref, gu_ref, rk_ref, lnw_ref, lnb_ref, yf_ref,
         out_ref, xpad_s, r_s, kap_s, v_s, kt_s, b_s, lw_s, y_s, bon_s, g_s, st_s) = refs
    else:
        (x_ref, xp_ref, xn_ref, mu_ref, kk_ref, ka_ref, d0_ref, du_ref, a0_ref, au_ref,
         out_ref, xpad_s, r_s, kap_s, v_s, kt_s, b_s, lw_s, st_s) = refs
    TB = TOKEN_BLOCK
    C = RW_CHUNK
    P = 2 * RW_HD
    D = RW_HEADS * RW_HD
    j = pl.program_id(1)
    blk = _bwd_block(j, nc, nb) if reverse else j

    @pl.when(j == 0)
    def _():
        st_s[...] = jnp.zeros_like(st_s)

    seg_start = (blk == 0) | (blk == nc)
    seg_end = (blk == nc - 1) | (blk == nb - 1)
    xpad_s[0:HALO, :] = jnp.where(seg_start, 0.0, xp_ref[0])
    xpad_s[HALO:HALO + TB, :] = x_ref[0]
    xpad_s[HALO + TB:, :] = jnp.where(seg_end, 0.0, xn_ref[0])
    x = x_ref[0]
    nbr = 0.5 * (xpad_s[HALO - 1:HALO - 1 + TB, :] + xpad_s[HALO + 1:HALO + 1 + TB, :])
    seg = x + mu_ref[...] * (nbr - x)
    r = seg[:, 0:D]
    k = seg[:, D:2 * D]
    v = seg[:, 2 * D:3 * D]
    dd = seg[:, 3 * D:3 * D + RW_RANK]
    ad = seg[:, 3 * D + RW_RANK:3 * D + 2 * RW_RANK]
    gd = seg[:, 3 * D + 2 * RW_RANK:3 * D + 3 * RW_RANK]

    lane = lax.broadcasted_iota(jnp.int32, (P, P), 1)
    sub = lax.broadcasted_iota(jnp.int32, (P, P), 0)
    same_head = (lane < RW_HD) == (sub < RW_HD)
    ones_bd = same_head.astype(BF16)

    logw = -RW_DECAY_SCALE * jax.nn.sigmoid(d0_ref[...] + _dot(jnp.tanh(dd).astype(BF16), du_ref[...]))
    ad_b = ad.astype(BF16)
    a = jax.nn.sigmoid(a0_ref[...] + _dot(ad_b, au_ref[...]))
    kk = k * kk_ref[...]
    kt = k * (1.0 + (a - 1.0) * ka_ref[...])
    if reverse:
        a_f = jax.nn.sigmoid(a0f_ref[...] + _dot(ad_b, auf_ref[...]))
        bon = r * (kt + k * (1.0 + (a_f - 1.0) * ka_ref[...])) * rk_ref[...]
        g_out = _dot(jax.nn.sigmoid(gd).astype(BF16), gu_ref[...])
    for p in range(RW_PAIRS):
        sl = slice(p * P, (p + 1) * P)
        if reverse:
            bon_s[p] = bon[:, sl]
            g_s[p] = g_out[:, sl]
        kkp = kk[:, sl]
        nrm = jnp.maximum(jnp.sqrt(_group_sum(kkp * kkp, ones_bd)), 1e-12)
        kap = kkp / nrm
        r_s[p] = r[:, sl]
        v_s[p] = v[:, sl]
        kap_s[p] = kap
        kt_s[p] = kt[:, sl]
        b_s[p] = kap * a[:, sl]
        lw_s[p] = logw[:, sl]

    row = lax.broadcasted_iota(jnp.int32, (C, C), 0)
    col = lax.broadcasted_iota(jnp.int32, (C, C), 1)
    incl = (col >= row) if reverse else (col <= row)
    strict = (col > row) if reverse else (col < row)
    tri = incl.astype(BF16)
    eye = (row == col).astype(F32)
    merge_masks = []
    for i in range(int(math.log2(C))):
        same_parent = jnp.right_shift(row, i + 1) == jnp.right_shift(col, i + 1)
        merge_masks.append(same_parent & (jnp.right_shift(row, i) != jnp.right_shift(col, i)))
    lane_lo = lax.broadcasted_iota(jnp.int32, (1, P), 1) < RW_HD
    head_masks = (lane_lo, jnp.logical_not(lane_lo))
    lane2 = lax.broadcasted_iota(jnp.int32, (1, 2 * P), 1)
    lane_lo2 = jnp.bitwise_and(lane2, P - 1) < RW_HD
    left2 = lane2 < C
    eye2 = jnp.concatenate([eye, eye], axis=1)
    merge_masks2 = [jnp.concatenate([mm, mm], axis=1) for mm in merge_masks]
    mid = C // 2
    n_chunks = TB // C
    order = range(n_chunks - 1, -1, -1) if reverse else range(n_chunks)

    def group_body(gi, carry):
        pairs = [gi * RW_GROUP + q for q in range(RW_GROUP)]
        items = [(q, c) for q in range(RW_GROUP) for c in range(n_chunks)]
        pre = {}
        for q, c in items:
            p = pairs[q]
            rows = slice(c * C, (c + 1) * C)
            lw = lw_s[p, rows, :]
            rr = r_s[p, rows, :]
            kap = kap_s[p, rows, :]
            ktc = kt_s[p, rows, :]
            bb = b_s[p, rows, :]
            L = _tri_sum_left(tri, lw)
            Lx = L - lw
            Lm = L[mid:mid + 1, :]
            Lc = jnp.sum(lw, axis=0, keepdims=True)
            e_neg = jnp.exp(Lm - L)
            e_end = jnp.exp(Lc - L)
            xs = jnp.concatenate([kap * jnp.exp(Lx - Lm), rr * jnp.exp(L - Lm)], axis=0)
            ys = jnp.concatenate([bb * e_neg, ktc * e_neg], axis=0).astype(BF16)
            d = dict(
                kap_st=(kap * jnp.exp(Lx)).astype(BF16),
                r_st=(rr * jnp.exp(L)).astype(BF16),
                bk_end_t=jnp.concatenate([(bb * e_end).T, (ktc * e_end).T], axis=1).astype(BF16),
                vb=v_s[p, rows, :].astype(BF16),
                decay_t=jnp.exp(jnp.broadcast_to(Lc, (P, P)).T),
                heads=[],
            )
            for hm in head_masks:
                g = _dot_nt(jnp.where(hm, xs, 0.0).astype(BF16), ys)
                d["heads"].append(dict(
                    a=jnp.where(strict, g[:C, :C], 0.0),
                    bm_rk=jnp.concatenate([jnp.where(strict, g[:C, C:], 0.0),
                                           jnp.where(incl, g[C:, C:], 0.0)], axis=0).astype(BF16),
                    rb=jnp.where(incl, g[C:, :C], 0.0).astype(BF16),
                ))
            pre[(q, c)] = d

        chains = [pre[it]["heads"][h] for it in items for h in range(2)]
        for ch in chains:
            ch["t"] = eye - jnp.where(merge_masks[0], ch["a"], 0.0)
        for mm in merge_masks[1:]:
            for ch in chains:
                ch["tb"] = ch["t"].astype(BF16)
                ch["m1"] = _dot(jnp.where(mm, ch["a"], 0.0).astype(BF16), ch["tb"]).astype(BF16)
            for ch in chains:
                ch["t"] = ch["t"] - _dot(ch["tb"], ch["m1"])

        for it in items:
            d = pre[it]
            h0, h1 = d["heads"]
            d["t2"] = jnp.concatenate([h0["t"], h1["t"]], axis=1)
            x = _dot(jnp.concatenate([h0["bm_rk"], h1["bm_rk"]], axis=1), _head_stack(d["vb"], lane_lo))
            d["rkv"] = x[C:]
            rhs = jnp.concatenate([d["kap_st"], x[:C].astype(BF16)], axis=1)
            tk = _dot(d["t2"].astype(BF16), _head_stack(rhs, lane_lo2))
            d["w_st"] = tk[:, :P].astype(BF16)
            d["u_free"] = tk[:, P:]
            d["rb2"] = jnp.concatenate([h0["rb"], h1["rb"]], axis=1)

        for q in range(RW_GROUP):
            p = pairs[q]
            st = st_s[p]
            for c in order:
                d = pre[(q, c)]
                rows = slice(c * C, (c + 1) * C)
                ws_rs = _dot(jnp.concatenate([d["w_st"], d["r_st"]], axis=0), st.astype(BF16))
                ub = (-(ws_rs[:C] + d["u_free"])).astype(BF16)
                y = ws_rs[C:] + d["rkv"] + _dot(d["rb2"], _head_stack(ub, lane_lo))
                upd = _dot(d["bk_end_t"], jnp.concatenate([ub, d["vb"]], axis=0))
                st = st * d["decay_t"] + jnp.where(same_head, upd, 0.0)
                if reverse:
                    y_s[p, rows, :] = y
                else:
                    out_ref[0, p, rows, :] = y
            st_s[p] = st
        return carry

    lax.fori_loop(0, RW_PAIRS // RW_GROUP, group_body, 0)

    if reverse:
        inv = 1.0 / RW_HD
        for p in range(RW_PAIRS):
            sl = slice(p * P, (p + 1) * P)
            y = yf_ref[0, p] + y_s[p]
            mu = _group_sum(y, ones_bd) * inv
            dlt = y - mu
            var = _group_sum(dlt * dlt, ones_bd) * inv
            yn = dlt * lax.rsqrt(var + RW_LN_EPS) * lnw_ref[:, sl] + lnb_ref[:, sl]
            bonus = _group_sum(bon_s[p], ones_bd) * v_s[p]
            out_ref[0, :, sl] = ((yn + bonus) * g_s[p]).astype(out_ref.dtype)


def _rwkv_serial(x, mu, k_k, k_a, d0, du, a0, au, gate_up, r_k, ln_w, ln_b, nc):
    B, TA, WS = x.shape
    D = RW_HEADS * RW_HD
    P = 2 * RW_HD
    nb = TA // TOKEN_BLOCK
    slab = lambda: pltpu.VMEM((RW_PAIRS, TOKEN_BLOCK, P), F32)
    const = lambda shape: pl.BlockSpec(shape, lambda b, j: (0,) * len(shape))
    row = lambda a: a.reshape(1, -1)

    def run(d, extra_in, extra_specs, out_spec, out_shape, extra_scratch):
        reverse = d == 1
        blk = _block_fn(reverse, nc, nb)
        tok = pl.BlockSpec((1, TOKEN_BLOCK, WS), lambda b, j: (b, blk(j), 0))
        prev, nxt = _halo_specs(WS, blk, TA)
        in_specs = [tok, prev, nxt, const((1, WS)), const((1, D)), const((1, D)),
                    const((1, D)), const((RW_RANK, D)), const((1, D)), const((RW_RANK, D))] + extra_specs(blk)
        scratch = [pltpu.VMEM((TOKEN_BLOCK + 2 * HALO, WS), F32)] + [slab() for _ in range(6)] + extra_scratch + [
            pltpu.VMEM((RW_PAIRS, P, P), F32)]
        return pl.pallas_call(
            functools.partial(_rwkv_kernel, reverse, nc, nb),
            grid=(B, nb),
            in_specs=in_specs,
            out_specs=out_spec(blk),
            out_shape=out_shape,
            scratch_shapes=scratch,
            compiler_params=_cparams("arbitrary", "arbitrary"),
        )(x, x, x, row(mu), row(k_k), row(k_a), d0[d:d + 1], du[d], a0[d:d + 1], au[d], *extra_in)

    pair_spec = lambda blk: pl.BlockSpec((1, RW_PAIRS, TOKEN_BLOCK, P), lambda b, j: (b, 0, blk(j), 0))
    yf = run(0, (), lambda blk: [], pair_spec, jax.ShapeDtypeStruct((B, RW_PAIRS, TA, P), F32), [])
    return run(
        1,
        (a0[0:1], au[0], gate_up, row(r_k), row(ln_w), row(ln_b), yf),
        lambda blk: [const((1, D)), const((RW_RANK, D)), const((RW_RANK, D)), const((1, D)), const((1, D)),
                     const((1, D)), pair_spec(blk)],
        lambda blk: pl.BlockSpec((1, TOKEN_BLOCK, D), lambda b, j: (b, blk(j), 0)),
        jax.ShapeDtypeStruct((B, TA, D), BF16),
        [slab(), slab(), slab()],
    )


def _rwkv_kernel(reverse, nc, nb, *refs):
    if reverse:
        (x_ref, xp_ref, xn_ref, mu_ref, kk_ref, ka_ref, d0_ref, du_ref, a0_ref, au_ref,
         a0f_ref, auf_ref, gu_ref, rk_ref, lnw_ref, lnb_ref, yf_ref,
         out_ref, xpad_s, xs_s, ys_s, op_s, bk_s, dec_s, v_s, bon_s, g_s, st_s) = refs
        handoff = (xs_s, ys_s, op_s, bk_s, dec_s, v_s, bon_s, g_s)
    else:
        (x_ref, xp_ref, xn_ref, mu_ref, kk_ref, ka_ref, d0_ref, du_ref, a0_ref, au_ref,
         out_ref, xpad_s, xs_s, ys_s, op_s, bk_s, dec_s, st_s) = refs
        handoff = (xs_s, ys_s, op_s, bk_s, dec_s)
    TB = TOKEN_BLOCK
    C = RW_CHUNK
    P = 2 * RW_HD
    D = RW_HEADS * RW_HD
    n_chunks = TB // C
    j = pl.program_id(1)
    wr = lax.rem(j, 2)
    rd = 1 - wr
    jin = jnp.minimum(j, nb - 1)
    blk = _bwd_block(jin, nc, nb) if reverse else jin

    @pl.when(j == 0)
    def _():
        st_s[...] = jnp.zeros_like(st_s)
        for ref in handoff:
            ref[1] = jnp.zeros(ref.shape[1:], ref.dtype)

    lane = lax.broadcasted_iota(jnp.int32, (P, P), 1)
    sub = lax.broadcasted_iota(jnp.int32, (P, P), 0)
    same_head = (lane < RW_HD) == (sub < RW_HD)
    ones_bd = same_head.astype(BF16)
    row = lax.broadcasted_iota(jnp.int32, (C, C), 0)
    col = lax.broadcasted_iota(jnp.int32, (C, C), 1)
    incl = (col >= row) if reverse else (col <= row)
    strict = (col > row) if reverse else (col < row)
    tri = incl.astype(BF16)
    eye = (row == col).astype(F32)
    merge_masks = []
    for i in range(int(math.log2(C))):
        same_parent = jnp.right_shift(row, i + 1) == jnp.right_shift(col, i + 1)
        merge_masks.append(same_parent & (jnp.right_shift(row, i) != jnp.right_shift(col, i)))
    lane_lo = lax.broadcasted_iota(jnp.int32, (1, P), 1) < RW_HD
    head_masks = (lane_lo, jnp.logical_not(lane_lo))
    lane2 = lax.broadcasted_iota(jnp.int32, (1, 2 * P), 1)
    lane_lo2 = jnp.bitwise_and(lane2, P - 1) < RW_HD
    mid = C // 2
    order = range(n_chunks - 1, -1, -1) if reverse else range(n_chunks)
    items = [(p, c) for p in range(RW_PAIRS) for c in range(n_chunks)]
    slot_of = lambda p, c: p * n_chunks + c

    pre = {}

    def consume_gram():
        for p, c in items:
            i = slot_of(p, c)
            ys = ys_s[rd, i]
            d = dict(kap_st=op_s[rd, 3 * i], r_st=op_s[rd, 3 * i + 1], vb=op_s[rd, 3 * i + 2],
                     bk=bk_s[rd, i], dec=dec_s[rd, i], heads=[])
            for h in range(2):
                g = _dot_nt(xs_s[rd, 2 * i + h], ys)
                a = jnp.where(strict, g[:C, :C], 0.0)
                d["heads"].append(dict(
                    ab=a.astype(BF16),
                    t=eye - jnp.where(merge_masks[0], a, 0.0),
                    bm_rk=jnp.concatenate([jnp.where(strict, g[:C, C:], 0.0),
                                           jnp.where(incl, g[C:, C:], 0.0)], axis=0).astype(BF16),
                    rb=jnp.where(incl, g[C:, :C], 0.0).astype(BF16),
                ))
            pre[(p, c)] = d

    zero_b = jnp.zeros((C, C), BF16)

    def inverse_level_a(mm):
        for it in items:
            for ch in pre[it]["heads"]:
                ch["tb"] = ch["t"].astype(BF16)
                ch["m1"] = _dot(jnp.where(mm, ch["ab"], zero_b), ch["tb"]).astype(BF16)

    def inverse_level_b():
        for it in items:
            for ch in pre[it]["heads"]:
                ch["t"] = ch["t"] - _dot(ch["tb"], ch["m1"])

    def consume_apply():
        for it in items:
            d = pre[it]
            h0, h1 = d["heads"]
            d["x"] = _dot(jnp.concatenate([h0["bm_rk"], h1["bm_rk"]], axis=1), _head_stack(d["vb"], lane_lo))
        for it in items:
            d = pre[it]
            h0, h1 = d["heads"]
            t2 = jnp.concatenate([h0["t"], h1["t"]], axis=1).astype(BF16)
            rhs = jnp.concatenate([d["kap_st"], d["x"][:C].astype(BF16)], axis=1)
            d["tk"] = _dot(t2, _head_stack(rhs, lane_lo2))
        for it in items:
            d = pre[it]
            h0, h1 = d["heads"]
            d["w_st"] = d["tk"][:, :P].astype(BF16)
            d["rb2"] = jnp.concatenate([h0["rb"], h1["rb"]], axis=1)

    def consume_recurrence():
        inv = 1.0 / RW_HD
        pairs = range(RW_PAIRS)
        st = {p: st_s[p] for p in pairs}
        ys_out = {}
        for c in order:
            ws_rs, ub, upd = {}, {}, {}
            for p in pairs:
                d = pre[(p, c)]
                ws_rs[p] = _dot(jnp.concatenate([d["w_st"], d["r_st"]], axis=0), st[p].astype(BF16))
            for p in pairs:
                ub[p] = (-(ws_rs[p][:C] + pre[(p, c)]["tk"][:, P:])).astype(BF16)
            for p in pairs:
                d = pre[(p, c)]
                upd[p] = _dot(d["bk"], jnp.concatenate([ub[p], d["vb"]], axis=0))
            for p in pairs:
                st[p] = st[p] * pre[(p, c)]["dec"] + jnp.where(same_head, upd[p], 0.0)
            for p in pairs:
                d = pre[(p, c)]
                ys_out[(p, c)] = ws_rs[p][C:] + d["x"][C:] + _dot(d["rb2"], _head_stack(ub[p], lane_lo))
        for p in pairs:
            st_s[p] = st[p]
        if reverse:
            y = {p: yf_ref[0, p] + jnp.concatenate([ys_out[(p, c)] for c in range(n_chunks)], axis=0) for p in pairs}
            dlt = {p: y[p] - _group_sum(y[p], ones_bd) * inv for p in pairs}
            var = {p: _group_sum(dlt[p] * dlt[p], ones_bd) * inv for p in pairs}
            bonus = {p: _group_sum(bon_s[rd, p], ones_bd) * v_s[rd, p] for p in pairs}
            for p in pairs:
                sl = slice(p * P, (p + 1) * P)
                yn = dlt[p] * lax.rsqrt(var[p] + RW_LN_EPS) * lnw_ref[:, sl] + lnb_ref[:, sl]
                out_ref[0, :, sl] = ((yn + bonus[p]) * g_s[rd, p]).astype(out_ref.dtype)
        else:
            for p in pairs:
                for c in range(n_chunks):
                    out_ref[0, p, c * C:(c + 1) * C, :] = ys_out[(p, c)]

    def shifted(cols):
        m = mu_ref[:, cols]
        nbr = xpad_s[HALO - 1:HALO - 1 + TB, cols] + xpad_s[HALO + 1:HALO + 1 + TB, cols]
        return xpad_s[HALO:HALO + TB, cols] * (1.0 - m) + nbr * (0.5 * m)

    low = {}

    def prepare_common():
        seg_start = (blk == 0) | (blk == nc)
        seg_end = (blk == nc - 1) | (blk == nb - 1)
        xpad_s[0:HALO, :] = jnp.where(seg_start, 0.0, xp_ref[0])
        xpad_s[HALO:HALO + TB, :] = x_ref[0]
        xpad_s[HALO + TB:, :] = jnp.where(seg_end, 0.0, xn_ref[0])
        low["dd"] = jnp.tanh(shifted(slice(3 * D, 3 * D + RW_RANK))).astype(BF16)
        low["ad"] = shifted(slice(3 * D + RW_RANK, 3 * D + 2 * RW_RANK)).astype(BF16)
        if reverse:
            low["gd"] = jax.nn.sigmoid(shifted(slice(3 * D + 2 * RW_RANK, 3 * D + 3 * RW_RANK))).astype(BF16)

    def prepare_pair(p):
        sl = slice(p * P, (p + 1) * P)
        r = shifted(slice(p * P, (p + 1) * P))
        k = shifted(slice(D + p * P, D + (p + 1) * P))
        v = shifted(slice(2 * D + p * P, 2 * D + (p + 1) * P))
        logw = -RW_DECAY_SCALE * jax.nn.sigmoid(d0_ref[:, sl] + _dot(low["dd"], du_ref[:, sl]))
        a = jax.nn.sigmoid(a0_ref[:, sl] + _dot(low["ad"], au_ref[:, sl]))
        kt = k * (1.0 + (a - 1.0) * ka_ref[:, sl])
        if reverse:
            a_f = jax.nn.sigmoid(a0f_ref[:, sl] + _dot(low["ad"], auf_ref[:, sl]))
            bon_s[wr, p] = r * (kt + k * (1.0 + (a_f - 1.0) * ka_ref[:, sl])) * rk_ref[:, sl]
            g_s[wr, p] = _dot(low["gd"], gu_ref[:, sl])
            v_s[wr, p] = v
        kkp = k * kk_ref[:, sl]
        kap_p = kkp / jnp.maximum(jnp.sqrt(_group_sum(kkp * kkp, ones_bd)), 1e-12)
        b_p = kap_p * a
        for c in range(n_chunks):
            i = slot_of(p, c)
            rows = slice(c * C, (c + 1) * C)
            lw = logw[rows]
            rr = r[rows]
            kap = kap_p[rows]
            ktc = kt[rows]
            bb = b_p[rows]
            L = _tri_sum_left(tri, lw)
            Lx = L - lw
            Lm = L[mid:mid + 1, :]
            Lc = jnp.sum(lw, axis=0, keepdims=True)
            e_neg = jnp.exp(Lm - L)
            e_end = jnp.exp(Lc - L)
            xs = jnp.concatenate([kap * jnp.exp(Lx - Lm), rr * jnp.exp(L - Lm)], axis=0)
            for h, hm in enumerate(head_masks):
                xs_s[wr, 2 * i + h] = jnp.where(hm, xs, 0.0).astype(BF16)
            ys_s[wr, i] = jnp.concatenate([bb * e_neg, ktc * e_neg], axis=0).astype(BF16)
            op_s[wr, 3 * i] = (kap * jnp.exp(Lx)).astype(BF16)
            op_s[wr, 3 * i + 1] = (rr * jnp.exp(L)).astype(BF16)
            op_s[wr, 3 * i + 2] = v[rows].astype(BF16)
            bk_s[wr, i] = jnp.concatenate([(bb * e_end).T, (ktc * e_end).T], axis=1).astype(BF16)
            dec_s[wr, i] = jnp.exp(jnp.broadcast_to(Lc, (P, P)).T)

    prepare_common()
    consume_gram()
    todo = list(range(RW_PAIRS))
    for mm in merge_masks[1:]:
        inverse_level_a(mm)
        if todo:
            prepare_pair(todo.pop(0))
        inverse_level_b()
    consume_apply()
    if todo:
        prepare_pair(todo.pop(0))
    consume_recurrence()
    for p in todo:
        prepare_pair(p)


def _rwkv(x, mu, k_k, k_a, d0, du, a0, au, gate_up, r_k, ln_w, ln_b, nc):
    B, TA, WS = x.shape
    D = RW_HEADS * RW_HD
    P = 2 * RW_HD
    C = RW_CHUNK
    nb = TA // TOKEN_BLOCK
    n_items = RW_PAIRS * (TOKEN_BLOCK // C)
    const = lambda shape: pl.BlockSpec(shape, lambda b, j: (0,) * len(shape))
    row = lambda a: a.reshape(1, -1)
    slab = lambda: pltpu.VMEM((2, RW_PAIRS, TOKEN_BLOCK, P), F32)

    def run(d, extra_in, extra_specs, out_spec, out_shape, extra_scratch):
        reverse = d == 1
        blk = _block_fn(reverse, nc, nb)
        blk_in = lambda j: blk(jnp.minimum(j, nb - 1))
        blk_out = lambda j: blk(jnp.maximum(j - 1, 0))
        tok = pl.BlockSpec((1, TOKEN_BLOCK, WS), lambda b, j: (b, blk_in(j), 0))
        prev, nxt = _halo_specs(WS, blk_in, TA)
        in_specs = [tok, prev, nxt, const((1, WS)), const((1, D)), const((1, D)),
                    const((1, D)), const((RW_RANK, D)), const((1, D)), const((RW_RANK, D))] + extra_specs(blk_out)
        scratch = [
            pltpu.VMEM((TOKEN_BLOCK + 2 * HALO, WS), F32),
            pltpu.VMEM((2, 2 * n_items, 2 * C, P), BF16),
            pltpu.VMEM((2, n_items, 2 * C, P), BF16),
            pltpu.VMEM((2, 3 * n_items, C, P), BF16),
            pltpu.VMEM((2, n_items, P, 2 * C), BF16),
            pltpu.VMEM((2, n_items, P, P), F32),
        ] + extra_scratch + [pltpu.VMEM((RW_PAIRS, P, P), F32)]
        return pl.pallas_call(
            functools.partial(_rwkv_kernel, reverse, nc, nb),
            grid=(B, nb + 1),
            in_specs=in_specs,
            out_specs=out_spec(blk_out),
            out_shape=out_shape,
            scratch_shapes=scratch,
            compiler_params=_cparams("arbitrary", "arbitrary"),
        )(x, x, x, row(mu), row(k_k), row(k_a), d0[d:d + 1], du[d], a0[d:d + 1], au[d], *extra_in)

    pair_spec = lambda blk: pl.BlockSpec((1, RW_PAIRS, TOKEN_BLOCK, P), lambda b, j: (b, 0, blk(j), 0))
    yf = run(0, (), lambda blk: [], pair_spec, jax.ShapeDtypeStruct((B, RW_PAIRS, TA, P), F32), [])
    return run(
        1,
        (a0[0:1], au[0], gate_up, row(r_k), row(ln_w), row(ln_b), yf),
        lambda blk: [const((1, D)), const((RW_RANK, D)), const((RW_RANK, D)), const((1, D)), const((1, D)),
                     const((1, D)), pair_spec(blk)],
        lambda blk: pl.BlockSpec((1, TOKEN_BLOCK, D), lambda b, j: (b, blk(j), 0)),
        jax.ShapeDtypeStruct((B, TA, D), BF16),
        [slab(), slab(), slab()],
    )


def _merge_kernel(x_ref, yml_ref, ylru_ref, yrw_ref, g_ref, oml_ref, olru_ref, orw_ref, wout_ref,
                  mod_ref, nw_ref, xo_ref, h_ref):
    D = x_ref.shape[-1]
    g = g_ref[0]
    acc = jax.nn.sigmoid(g[:, 0:D]) * _dot(yml_ref[0], oml_ref[...])
    acc = acc + jax.nn.sigmoid(g[:, D:2 * D]) * _dot(ylru_ref[0], olru_ref[...])
    acc = acc + jax.nn.sigmoid(g[:, 2 * D:3 * D]) * _dot(yrw_ref[0], orw_ref[...])
    y = _dot(acc.astype(BF16), wout_ref[...])
    m = mod_ref[0]
    xn = x_ref[0] + m[2:3] * y
    xo_ref[0] = xn
    h_ref[0] = (_rms(xn, nw_ref[...]) * (1.0 + m[4:5]) + m[3:4]).astype(BF16)


def _merge(x, yml, ylru, yrw, gates, oml, olru, orw, wout, mod, norm2_w, nc, skip_ctx):
    B, TA, D = x.shape
    nb = TA // TOKEN_BLOCK
    off = nc if skip_ctx else 0
    rows_out = (nb - off) * TOKEN_BLOCK
    tok = lambda w: pl.BlockSpec((1, TOKEN_BLOCK, w), lambda b, j: (b, j + off, 0))
    otok = pl.BlockSpec((1, TOKEN_BLOCK, D), lambda b, j: (b, j, 0))
    wspec = pl.BlockSpec((D, D), lambda b, j: (0, 0))
    return pl.pallas_call(
        _merge_kernel,
        grid=(B, nb - off),
        in_specs=[tok(D), tok(D), tok(D), tok(D), tok(3 * D), wspec, wspec, wspec, wspec,
                  pl.BlockSpec((1, 6, D), lambda b, j: (jnp.where(j + off < nc, B, b), 0, 0)),
                  pl.BlockSpec((1, D), lambda b, j: (0, 0))],
        out_specs=[otok, otok],
        out_shape=[jax.ShapeDtypeStruct((B, rows_out, D), F32), jax.ShapeDtypeStruct((B, rows_out, D), BF16)],
        input_output_aliases={} if skip_ctx else {0: 0},
        compiler_params=_cparams("arbitrary", "arbitrary"),
    )(x, yml, ylru, yrw, gates, oml, olru, orw, wout, mod, norm2_w.reshape(1, D))


def _ffn_kernel(x_ref, h_ref, win_ref, wout_ref, mod_ref, xo_ref):
    F = wout_ref.shape[0]
    u = _dot(h_ref[0], win_ref[...])
    gate = u[:, :F]
    act = (gate * jax.nn.sigmoid(gate) * u[:, F:]).astype(BF16)
    xo_ref[0] = x_ref[0] + mod_ref[0][5:6] * _dot(act, wout_ref[...])


def _ffn(x, h, win, wout, mod, nc):
    B, TA, D = x.shape
    F = wout.shape[0]
    nb = TA // TOKEN_BLOCK
    tok = pl.BlockSpec((1, TOKEN_BLOCK, D), lambda b, j: (b, j, 0))
    return pl.pallas_call(
        _ffn_kernel,
        grid=(B, nb),
        in_specs=[tok, tok,
                  pl.BlockSpec((D, 2 * F), lambda b, j: (0, 0)),
                  pl.BlockSpec((F, D), lambda b, j: (0, 0)),
                  pl.BlockSpec((1, 6, D), lambda b, j: (jnp.where(j < nc, B, b), 0, 0))],
        out_specs=tok,
        out_shape=jax.ShapeDtypeStruct((B, TA, D), F32),
        input_output_aliases={0: 0},
        compiler_params=_cparams("arbitrary", "arbitrary"),
    )(x, h, win, wout, mod)


def _grid_transpose_latent(xa, ctx_len, rows, cols):
    B, TA, D = xa.shape
    lat = xa[:, ctx_len:].reshape(B, rows, cols, D).transpose(0, 2, 1, 3).reshape(B, rows * cols, D)
    return jnp.concatenate([xa[:, :ctx_len], lat], axis=1)


def _block_diag_groups(w):
    per = LRU_GROUP // LRU_BD
    d2 = w.shape[0]
    wg = w.reshape(d2, LRU_BLOCKS // per, per, LRU_BD, LRU_BD)
    eye = jnp.eye(per, dtype=w.dtype)
    out = jnp.einsum('dgpio,pq->dgpiqo', wg, eye)
    return out.reshape(d2, LRU_BLOCKS // per, LRU_GROUP, LRU_GROUP).astype(BF16)


def _pad_dir_rows(w):
    z = jnp.zeros_like(w[0])
    return jnp.stack([jnp.concatenate([w[0], z], axis=0), jnp.concatenate([z, w[1]], axis=0)]).astype(BF16)


def kernel(x, c, ctx, c_ctx, w_mod, b_mod, norm1_w, norm2_w, w_in, ml_ig_b, ml_fg_b, ml_norm_w, lru_conv_w, lru_conv_b, lru_gr_w, lru_gr_b, lru_gi_w, lru_gi_b, lru_lambda, rw_mu, rw_decay0, rw_decay_up, rw_iclr0, rw_iclr_up, rw_gate_up, rw_k_k, rw_k_a, rw_r_k, rw_ln_w, rw_ln_b, out_ml, out_lru, out_rw, w_out, w_ffn_in, w_ffn_out, final_norm_w):
    B, T, D = x.shape
    CTX = ctx.shape[1]
    L = w_mod.shape[0]
    TA = CTX + T
    assert CTX % TOKEN_BLOCK == 0 and T % TOKEN_BLOCK == 0 and T % GRID_W == 0
    nc = CTX // TOKEN_BLOCK
    rows = T // GRID_W
    N = B * TA
    MLW = ML_HEADS * ML_HD

    BP = -(-(B + 1) // 8) * 8
    cond = jnp.concatenate([c, c_ctx[None, :], jnp.zeros((BP - B - 1, D), F32)], axis=0)
    mods = _adaln_all(cond, w_mod, b_mod).reshape(L, BP, 6, D)

    xa = jnp.concatenate([ctx, x], axis=1)
    column_order = False
    for layer in range(L):
        last = layer == L - 1
        want_columns = layer % 2 == 1
        if want_columns != column_order:
            xa = _grid_transpose_latent(xa, CTX, GRID_W if column_order else rows, rows if column_order else GRID_W)
            column_order = want_columns
        mod = mods[layer]

        wi = w_in[layer]
        o = 0
        wq = wi[:, o:o + MLW]; o += MLW
        wk = wi[:, o:o + MLW] * (ML_HD ** -0.5); o += MLW
        wv = wi[:, o:o + MLW]; o += MLW
        wo = wi[:, o:o + MLW]; o += MLW
        wif = wi[:, o:o + 4 * ML_HEADS]; o += 4 * ML_HEADS
        wlx = wi[:, o:o + D]; o += D
        wlg = wi[:, o:o + D]; o += D
        rw_w = 3 * D + 3 * RW_RANK
        wrw = wi[:, o:o + rw_w]; o += rw_w
        wmg = wi[:, o:o + 3 * D]; o += 3 * D
        w_qkv = jnp.concatenate([wq, wk, wv], axis=1).astype(BF16)
        w_if = jnp.concatenate([wif, jnp.zeros((D, 128 - 4 * ML_HEADS), F32)], axis=1).astype(BF16)

        h = _prenorm(xa, norm1_w[layer], mod, nc).reshape(N, D)
        qkv = _matmul(h, w_qkv, BF16, 1024).reshape(B, TA, 3 * MLW)
        ml_o = _matmul(h, wo.astype(BF16), F32, 1024).reshape(B, TA, MLW)
        gcol = _matmul(h, w_if, F32, 128).reshape(B, TA, 128)
        lru_x = _matmul(h, wlx.astype(BF16), F32, 1024).reshape(B, TA, D)
        lru_g = _matmul(h, wlg.astype(BF16), F32, 1024).reshape(B, TA, D)
        rw = _matmul(h, wrw.astype(BF16), F32, rw_w // 3).reshape(B, TA, rw_w)
        mgates = _matmul(h, wmg.astype(BF16), F32, 1024).reshape(B, TA, 3 * D)

        grow = gcol[:, :, :16].reshape(B, TA // ML_CHUNK, ML_CHUNK, 16).transpose(0, 1, 3, 2)
        gate_b = jnp.concatenate([ml_ig_b[layer].reshape(-1), ml_fg_b[layer].reshape(-1)])
        bcol = jnp.concatenate([gate_b, jnp.zeros((128 - 16,), F32)]).reshape(1, 128)
        y_ml = _mlstm(qkv, gcol, grow, bcol, gate_b.reshape(16, 1), ml_o, ml_norm_w[layer], nc)

        y_lru = _lru(lru_x, lru_g, lru_conv_w[layer], lru_conv_b[layer],
                     _block_diag_groups(lru_gr_w[layer]), _block_diag_groups(lru_gi_w[layer]),
                     lru_gr_b[layer], lru_gi_b[layer], lru_lambda[layer], nc)

        y_rw = _rwkv(rw, rw_mu[layer], rw_k_k[layer], rw_k_a[layer], rw_decay0[layer],
                     _pad_dir_rows(rw_decay_up[layer]), rw_iclr0[layer], _pad_dir_rows(rw_iclr_up[layer]),
                     rw_gate_up[layer].astype(BF16), rw_r_k[layer], rw_ln_w[layer], rw_ln_b[layer], nc)

        xa, h2 = _merge(xa, y_ml, y_lru, y_rw, mgates, out_ml[layer].astype(BF16), out_lru[layer].astype(BF16),
                        out_rw[layer].astype(BF16), w_out[layer].astype(BF16), mod, norm2_w[layer], nc, last)
        xa = _ffn(xa, h2, w_ffn_in[layer].astype(BF16), w_ffn_out[layer].astype(BF16), mod, 0 if last else nc)

    out = _final_norm(xa, final_norm_w, 0)
    if column_order:
        out = out.reshape(B, GRID_W, rows, D).transpose(0, 2, 1, 3).reshape(B, T, D)
    return out
```

```python
import functools
import math

import jax
import jax.numpy as jnp
from jax import lax
from jax.experimental import pallas as pl
from jax.experimental.pallas import tpu as pltpu

F32 = jnp.float32
BF16 = jnp.bfloat16

NORM_EPS = 1e-6
GRID_W = 64
TOKEN_BLOCK = 256
HALO = 8

ML_HEADS = 4
ML_HD = 256
ML_CHUNK = 64
ML_M_INIT = -1e30

LRU_BLOCKS = 16
LRU_BD = 64
LRU_CONV = 4
LRU_C = 8.0
LRU_GROUP = 256
RW_HD = 64
RW_HEADS = 16
RW_PAIRS = 8
RW_CHUNK = 128
RW_RANK = 128
RW_DECAY_SCALE = math.exp(-0.5)
RW_LN_EPS = 64e-5

VMEM_LIMIT = 56 * 1024 * 1024


def _cparams(*sem):
    return pltpu.CompilerParams(dimension_semantics=sem, vmem_limit_bytes=VMEM_LIMIT)


def _dot(a, b):
    return jnp.dot(a, b, preferred_element_type=F32)


def _dot_nt(a, b):
    return lax.dot_general(a, b, (((1,), (1,)), ((), ())), preferred_element_type=F32)


def _split2(x):
    h = x.astype(BF16)
    return h, (x - h.astype(F32)).astype(BF16)


def _tri_sum_left(tri, x):
    w = x.shape[1]
    y = _dot(tri, jnp.concatenate(_split2(x), axis=1))
    return y[:, :w] + y[:, w:]


def _tri_sum_right(x, tri):
    r = x.shape[0]
    y = _dot(jnp.concatenate(_split2(x), axis=0), tri)
    return y[:r] + y[r:]


def _group_sum(x, ones_bd):
    n = x.shape[0]
    h = x.astype(BF16)
    l = (x - h.astype(F32)).astype(BF16)
    y = _dot(jnp.concatenate([h, l], axis=0), ones_bd)
    return y[:n] + y[n:]


def _head_stack(z, head0):
    zero = jnp.zeros_like(z)
    return jnp.concatenate([jnp.where(head0, z, zero), jnp.where(head0, zero, z)], axis=0)


def _bwd_block(j, nc, nb):
    return jnp.where(j < nc, nc - 1 - j, nb + nc - 1 - j)


def _block_fn(reverse, nc, nb):
    if reverse:
        return lambda j: _bwd_block(j, nc, nb)
    return lambda j: j


def _mod_kernel(c_ref, w_ref, b_ref, o_ref):
    c = c_ref[...]
    s = (c * jax.nn.sigmoid(c)).astype(BF16)
    o_ref[0] = _dot(s, w_ref[0].astype(BF16)) + b_ref[0]


def _adaln_all(cond, w_mod, b_mod):
    L, D, D6 = w_mod.shape
    BP = cond.shape[0]
    tn = 1536
    return pl.pallas_call(
        _mod_kernel,
        grid=(L, D6 // tn),
        in_specs=[
            pl.BlockSpec((BP, D), lambda l, n: (0, 0)),
            pl.BlockSpec((1, D, tn), lambda l, n: (l, 0, n)),
            pl.BlockSpec((1, 1, tn), lambda l, n: (l, 0, n)),
        ],
        out_specs=pl.BlockSpec((1, BP, tn), lambda l, n: (l, 0, n)),
        out_shape=jax.ShapeDtypeStruct((L, BP, D6), F32),
        compiler_params=_cparams("arbitrary", "arbitrary"),
    )(cond, w_mod, b_mod.reshape(L, 1, D6))


def _rms(x, w):
    return x * lax.rsqrt(jnp.mean(x * x, axis=-1, keepdims=True) + NORM_EPS) * w


def _prenorm_kernel(x_ref, w_ref, mod_ref, h_ref):
    m = mod_ref[0]
    y = _rms(x_ref[0], w_ref[...])
    h_ref[0] = (y * (1.0 + m[1:2]) + m[0:1]).astype(BF16)


def _prenorm(x, norm_w, mod, nc):
    B, TA, D = x.shape
    nb = TA // TOKEN_BLOCK
    return pl.pallas_call(
        _prenorm_kernel,
        grid=(B, nb),
        in_specs=[
            pl.BlockSpec((1, TOKEN_BLOCK, D), lambda b, j: (b, j, 0)),
            pl.BlockSpec((1, D), lambda b, j: (0, 0)),
            pl.BlockSpec((1, 6, D), lambda b, j: (jnp.where(j < nc, B, b), 0, 0)),
        ],
        out_specs=pl.BlockSpec((1, TOKEN_BLOCK, D), lambda b, j: (b, j, 0)),
        out_shape=jax.ShapeDtypeStruct((B, TA, D), BF16),
        compiler_params=_cparams("arbitrary", "arbitrary"),
    )(x, norm_w.reshape(1, D), mod)


def _final_norm_kernel(x_ref, w_ref, o_ref):
    o_ref[0] = _rms(x_ref[0], w_ref[...])


def _final_norm(x, w, nc):
    B, TA, D = x.shape
    nl = TA // TOKEN_BLOCK - nc
    return pl.pallas_call(
        _final_norm_kernel,
        grid=(B, nl),
        in_specs=[
            pl.BlockSpec((1, TOKEN_BLOCK, D), lambda b, j: (b, j + nc, 0)),
            pl.BlockSpec((1, D), lambda b, j: (0, 0)),
        ],
        out_specs=pl.BlockSpec((1, TOKEN_BLOCK, D), lambda b, j: (b, j, 0)),
        out_shape=jax.ShapeDtypeStruct((B, nl * TOKEN_BLOCK, D), F32),
        compiler_params=_cparams("arbitrary", "arbitrary"),
    )(x, w.reshape(1, D))


def _mm_kernel(h_ref, w_ref, o_ref):
    o_ref[...] = _dot(h_ref[...], w_ref[...]).astype(o_ref.dtype)


def _pick(n, cands):
    for c in cands:
        if n % c == 0:
            return c
    raise ValueError(f"no block size for {n}")


def _matmul(h, w, out_dtype, tn):
    N, K = h.shape
    M = w.shape[1]
    tm = _pick(N, (1024, 512, 256))
    return pl.pallas_call(
        _mm_kernel,
        grid=(M // tn, N // tm),
        in_specs=[
            pl.BlockSpec((tm, K), lambda n, m: (m, 0)),
            pl.BlockSpec((K, tn), lambda n, m: (0, n)),
        ],
        out_specs=pl.BlockSpec((tm, tn), lambda n, m: (m, n)),
        out_shape=jax.ShapeDtypeStruct((N, M), out_dtype),
        compiler_params=_cparams("arbitrary", "arbitrary"),
    )(h, w)


def _mlstm_kernel(reverse, *refs):
    if reverse:
        (q_ref, k_ref, v_ref, kt_ref, gc_ref, gr_ref, bc_ref, br_ref, yf_ref, o_ref, nw_ref,
         out_ref, ct_ref, n_ref, m_ref) = refs
    else:
        (q_ref, k_ref, v_ref, kt_ref, gc_ref, gr_ref, bc_ref, br_ref,
         out_ref, ct_ref, n_ref, m_ref) = refs
    C = ML_CHUNK
    j = pl.program_id(1)

    @pl.when(j == 0)
    def _():
        ct_ref[...] = jnp.zeros_like(ct_ref)
        n_ref[...] = jnp.zeros_like(n_ref)
        m_ref[...] = jnp.full_like(m_ref, ML_M_INIT)

    row = lax.broadcasted_iota(jnp.int32, (C, C), 0)
    col = lax.broadcasted_iota(jnp.int32, (C, C), 1)
    causal = (col >= row) if reverse else (col <= row)
    tri_l = causal.astype(BF16)
    tri_r = ((row >= col) if reverse else (row <= col)).astype(BF16)
    last = 0 if reverse else C - 1
    d = 1 if reverse else 0

    n_chunks = TOKEN_BLOCK // C
    order = range(n_chunks - 1, -1, -1) if reverse else range(n_chunks)
    heads = range(ML_HEADS)
    items = [(c, h) for c in order for h in heads]
    lanes = lambda h: slice(h * ML_HD, (h + 1) * ML_HD)
    rows = lambda c: slice(c * C, (c + 1) * C)

    g = {}
    for c in order:
        gc = gc_ref[0, rows(c), :] + bc_ref[...]
        gr = gr_ref[0, c] + br_ref[...]
        fc = _tri_sum_left(tri_l, jax.nn.log_sigmoid(gc))
        fr = _tri_sum_right(jax.nn.log_sigmoid(gr), tri_r)
        for h in heads:
            gi = d * ML_HEADS + h
            g[(c, h)] = dict(i_c=gc[:, gi:gi + 1], f_c=fc[:, 8 + gi:9 + gi],
                             i_r=gr[gi:gi + 1, :], f_r=fr[8 + gi:9 + gi, :])
    for it in items:
        e = g[it]
        e["f_end"] = e["f_c"][last:last + 1, :]
        logw = jnp.where(causal, e["f_c"] - e["f_r"] + e["i_r"], -jnp.inf)
        e["m_loc"] = jnp.max(logw, axis=1, keepdims=True)
        e["w_loc"] = jnp.exp(logw - e["m_loc"])
        log_end_r = e["f_end"] - e["f_r"] + e["i_r"]
        e["m_end"] = jnp.max(log_end_r, axis=1, keepdims=True)
        e["w_end"] = jnp.exp(e["f_end"] - e["f_c"] + e["i_c"] - e["m_end"])
    for c, h in items:
        e = g[(c, h)]
        e["q"] = q_ref[0, rows(c), lanes(h)]
        e["s"] = _dot_nt(e["q"], k_ref[0, rows(c), lanes(h)]) * e["w_loc"]
    for c, h in items:
        e = g[(c, h)]
        e["num_loc"] = _dot(e["s"].astype(BF16), v_ref[0, rows(c), lanes(h)])
        e["den_loc"] = jnp.sum(e["s"], axis=1, keepdims=True)
    for c, h in items:
        e = g[(c, h)]
        wv = (e["w_end"] * v_ref[0, rows(c), lanes(h)].astype(F32)).astype(BF16)
        pad = [jnp.zeros((C, ML_HD), BF16)] * n_chunks
        pad[c] = wv
        e["kv_loc"] = _dot(kt_ref[0, lanes(h), :], jnp.concatenate(pad, axis=0))
        e["nk_loc"] = jnp.sum(e["w_end"] * k_ref[0, rows(c), lanes(h)].astype(F32), axis=0, keepdims=True)

    ct = {h: ct_ref[h] for h in heads}
    nv = {h: n_ref[h][0:1, :] for h in heads}
    ms = {h: m_ref[h][0:1, 0:1] for h in heads}
    h_out = {}
    for c in order:
        qc = {h: _dot(g[(c, h)]["q"], ct[h].astype(BF16)) for h in heads}
        for h in heads:
            e = g[(c, h)]
            inter = e["f_c"] + ms[h]
            m_t = jnp.maximum(e["m_loc"], inter)
            a_loc = jnp.exp(e["m_loc"] - m_t)
            w_inter = jnp.exp(inter - m_t)
            qn = jnp.sum(e["q"].astype(F32) * nv[h], axis=1, keepdims=True)
            num = a_loc * e["num_loc"] + w_inter * qc[h]
            den = a_loc * e["den_loc"] + w_inter * qn
            h_out[(c, h)] = num / jnp.maximum(jnp.abs(den), jnp.exp(-m_t))
        for h in heads:
            e = g[(c, h)]
            m_new = jnp.maximum(e["f_end"] + ms[h], e["m_end"])
            scale = jnp.exp(e["m_end"] - m_new)
            decay = jnp.exp(e["f_end"] + ms[h] - m_new)
            ct[h] = decay * ct[h] + scale * e["kv_loc"]
            nv[h] = decay * nv[h] + scale * e["nk_loc"]
            ms[h] = m_new
    for h in heads:
        ct_ref[h] = ct[h]
        n_ref[h] = jnp.broadcast_to(nv[h], n_ref.shape[1:])
        m_ref[h] = jnp.broadcast_to(ms[h], m_ref.shape[1:])

    for c, h in items:
        if reverse:
            tot = yf_ref[0, rows(c), lanes(h)] + h_out[(c, h)]
            hn = tot * lax.rsqrt(jnp.mean(tot * tot, axis=-1, keepdims=True) + NORM_EPS) * nw_ref[:, lanes(h)]
            out_ref[0, rows(c), lanes(h)] = (hn * jax.nn.sigmoid(o_ref[0, rows(c), lanes(h)])).astype(out_ref.dtype)
        else:
            out_ref[0, rows(c), lanes(h)] = h_out[(c, h)]


def _mlstm(qkv, gcol, grow, bcol, brow, o, norm_w, nc):
    B, TA, W3 = qkv.shape
    W = W3 // 3
    nb = TA // TOKEN_BLOCK
    cpb = TOKEN_BLOCK // ML_CHUNK
    scratch = [
        pltpu.VMEM((ML_HEADS, ML_HD, ML_HD), F32),
        pltpu.VMEM((ML_HEADS, 8, ML_HD), F32),
        pltpu.VMEM((ML_HEADS, 8, 128), F32),
    ]
    kt = jnp.swapaxes(qkv[:, :, W:2 * W], 1, 2)

    def run(reverse, extra_in, extra_specs, out_dtype):
        blk = _block_fn(reverse, nc, nb)
        tok = lambda w, cb: pl.BlockSpec((1, TOKEN_BLOCK, w), lambda b, j: (b, blk(j), cb))
        in_specs = [
            tok(W, 0), tok(W, 1), tok(W, 2),
            pl.BlockSpec((1, W, TOKEN_BLOCK), lambda b, j: (b, 0, blk(j))),
            pl.BlockSpec((1, TOKEN_BLOCK, 128), lambda b, j: (b, blk(j), 0)),
            pl.BlockSpec((1, cpb, 16, ML_CHUNK), lambda b, j: (b, blk(j), 0, 0)),
            pl.BlockSpec((1, 128), lambda b, j: (0, 0)),
            pl.BlockSpec((16, 1), lambda b, j: (0, 0)),
        ] + extra_specs(tok)
        return pl.pallas_call(
            functools.partial(_mlstm_kernel, reverse),
            grid=(B, nb),
            in_specs=in_specs,
            out_specs=tok(W, 0),
            out_shape=jax.ShapeDtypeStruct((B, TA, W), out_dtype),
            scratch_shapes=scratch,
            compiler_params=_cparams("arbitrary", "arbitrary"),
        )(qkv, qkv, qkv, kt, gcol, grow, bcol, brow, *extra_in)

    yf = run(False, (), lambda tok: [], F32)
    return run(True, (yf, o, norm_w.reshape(1, W)),
               lambda tok: [tok(W, 0), tok(W, 0), pl.BlockSpec((1, W), lambda b, j: (0, 0))], BF16)


def _halo_specs(width, blk, TA):
    per = TOKEN_BLOCK // HALO
    last = TA // HALO - 1
    prev = pl.BlockSpec((1, HALO, width), lambda b, j: (b, jnp.maximum(blk(j) * per - 1, 0), 0))
    nxt = pl.BlockSpec((1, HALO, width), lambda b, j: (b, jnp.minimum((blk(j) + 1) * per, last), 0))
    return prev, nxt


def _lru_kernel(reverse, nc, nb, *refs):
    if reverse:
        (x_ref, xp_ref, xn_ref, cw_ref, cb_ref, wr_ref, wi_ref, br_ref, bi_ref, lam_ref,
         hf_ref, gate_ref, out_ref, xpad_s, a_s, b_s, h_s) = refs
    else:
        (x_ref, xp_ref, xn_ref, cw_ref, cb_ref, wr_ref, wi_ref, br_ref, bi_ref, lam_ref,
         out_ref, xpad_s, a_s, b_s, h_s) = refs
    TB = TOKEN_BLOCK
    j = pl.program_id(1)
    blk = _bwd_block(j, nc, nb) if reverse else j

    @pl.when(j == 0)
    def _():
        h_s[...] = jnp.zeros_like(h_s)

    seg_start = (blk == 0) | (blk == nc)
    seg_end = (blk == nc - 1) | (blk == nb - 1)
    xpad_s[0:HALO, :] = jnp.where(seg_start, 0.0, xp_ref[0])
    xpad_s[HALO:HALO + TB, :] = x_ref[0]
    xpad_s[HALO + TB:, :] = jnp.where(seg_end, 0.0, xn_ref[0])
    left = LRU_CONV // 2
    u = cb_ref[...]
    for tap in range(LRU_CONV):
        off = HALO - left + tap
        u = u + cw_ref[tap:tap + 1, :] * xpad_s[off:off + TB, :]
    ub = u.astype(BF16)
    W = u.shape[1]
    r_parts, i_parts = [], []
    for g in range(W // LRU_GROUP):
        ug = ub[:, g * LRU_GROUP:(g + 1) * LRU_GROUP]
        r_parts.append(_dot(ug, wr_ref[g]))
        i_parts.append(_dot(ug, wi_ref[g]))
    r = jax.nn.sigmoid(jnp.concatenate(r_parts, axis=1) + br_ref[...])
    i = jax.nn.sigmoid(jnp.concatenate(i_parts, axis=1) + bi_ref[...])
    log_a = -LRU_C * r * jax.nn.softplus(-lam_ref[...])
    a = jnp.exp(log_a)
    a_s[...] = a
    b_s[...] = jnp.sqrt(jnp.maximum(1.0 - a * a, 0.0)) * (i * u)

    def step(t, h):
        idx = (TB - 1 - t) if reverse else t
        h = a_s[pl.ds(idx, 1), :] * h + b_s[pl.ds(idx, 1), :]
        b_s[pl.ds(idx, 1), :] = h
        return h

    h_s[...] = lax.fori_loop(0, TB, step, h_s[...], unroll=8)
    if reverse:
        out_ref[0] = ((hf_ref[0] + b_s[...]) * jax.nn.gelu(gate_ref[0])).astype(out_ref.dtype)
    else:
        out_ref[0] = b_s[...]


def _lru(x, gate, cw, cb, wr, wi, br, bi, lam, nc):
    B, TA, W = x.shape
    nb = TA // TOKEN_BLOCK
    ng = W // LRU_GROUP
    scratch = [
        pltpu.VMEM((TOKEN_BLOCK + 2 * HALO, W), F32),
        pltpu.VMEM((TOKEN_BLOCK, W), F32),
        pltpu.VMEM((TOKEN_BLOCK, W), F32),
        pltpu.VMEM((1, W), F32),
    ]

    def run(d, extra_in, out_dtype):
        reverse = d == 1
        blk = _block_fn(reverse, nc, nb)
        tok = pl.BlockSpec((1, TOKEN_BLOCK, W), lambda b, j: (b, blk(j), 0))
        prev, nxt = _halo_specs(W, blk, TA)
        const = lambda shape: pl.BlockSpec(shape, lambda b, j: (0,) * len(shape))
        in_specs = [tok, prev, nxt, const((LRU_CONV, W)), const((1, W)),
                    const((ng, LRU_GROUP, LRU_GROUP)), const((ng, LRU_GROUP, LRU_GROUP)),
                    const((1, W)), const((1, W)), const((1, W))] + [tok] * len(extra_in)
        return pl.pallas_call(
            functools.partial(_lru_kernel, reverse, nc, nb),
            grid=(B, nb),
            in_specs=in_specs,
            out_specs=tok,
            out_shape=jax.ShapeDtypeStruct((B, TA, W), out_dtype),
            scratch_shapes=scratch,
            compiler_params=_cparams("arbitrary", "arbitrary"),
        )(x, x, x, cw, cb.reshape(1, W), wr[d], wi[d], br[d:d + 1], bi[d:d + 1], lam[d:d + 1], *extra_in)

    hf = run(0, (), F32)
    return run(1, (hf, gate), BF16)


def _rwkv_kernel(reverse, nc, nb, *refs):
    if reverse:
        (x_ref, xp_ref, xn_ref, mu_ref, kk_ref, ka_ref, d0_ref, du_ref, a0_ref, au_ref,
         a0f_ref, auf_ref, gu_ref, rk_ref, lnw_ref, lnb_ref, yf_ref,
         out_ref, xpad_s, xs_s, ys_s, op_s, bk_s, dec_s, v_s, bon_s, g_s, st_s) = refs
        handoff = (xs_s, ys_s, op_s, bk_s, dec_s, v_s, bon_s, g_s)
    else:
        (x_ref, xp_ref, xn_ref, mu_ref, kk_ref, ka_ref, d0_ref, du_ref, a0_ref, au_ref,
         out_ref, xpad_s, xs_s, ys_s, op_s, bk_s, dec_s, st_s) = refs
        handoff = (xs_s, ys_s, op_s, bk_s, dec_s)
    TB = TOKEN_BLOCK
    C = RW_CHUNK
    P = 2 * RW_HD
    D = RW_HEADS * RW_HD
    n_chunks = TB // C
    j = pl.program_id(1)
    wr = lax.rem(j, 2)
    rd = 1 - wr
    jin = jnp.minimum(j, nb - 1)
    blk = _bwd_block(jin, nc, nb) if reverse else jin

    @pl.when(j == 0)
    def _():
        st_s[...] = jnp.zeros_like(st_s)
        for ref in handoff:
            ref[1] = jnp.zeros(ref.shape[1:], ref.dtype)

    lane = lax.broadcasted_iota(jnp.int32, (P, P), 1)
    sub = lax.broadcasted_iota(jnp.int32, (P, P), 0)
    same_head = (lane < RW_HD) == (sub < RW_HD)
    ones_bd = same_head.astype(BF16)
    row = lax.broadcasted_iota(jnp.int32, (C, C), 0)
    col = lax.broadcasted_iota(jnp.int32, (C, C), 1)
    incl = (col >= row) if reverse else (col <= row)
    strict = (col > row) if reverse else (col < row)
    tri = incl.astype(BF16)
    eye = (row == col).astype(F32)
    merge_masks = []
    for i in range(int(math.log2(C))):
        same_parent = jnp.right_shift(row, i + 1) == jnp.right_shift(col, i + 1)
        merge_masks.append(same_parent & (jnp.right_shift(row, i) != jnp.right_shift(col, i)))
    lane_lo = lax.broadcasted_iota(jnp.int32, (1, P), 1) < RW_HD
    head_masks = (lane_lo, jnp.logical_not(lane_lo))
    lane2 = lax.broadcasted_iota(jnp.int32, (1, 2 * P), 1)
    lane_lo2 = jnp.bitwise_and(lane2, P - 1) < RW_HD
    mid = C // 2
    order = range(n_chunks - 1, -1, -1) if reverse else range(n_chunks)
    items = [(p, c) for p in range(RW_PAIRS) for c in range(n_chunks)]
    slot_of = lambda p, c: p * n_chunks + c

    pre = {}

    def consume_gram():
        for p, c in items:
            i = slot_of(p, c)
            ys = ys_s[rd, i]
            d = dict(kap_st=op_s[rd, 3 * i], r_st=op_s[rd, 3 * i + 1], vb=op_s[rd, 3 * i + 2],
                     bk=bk_s[rd, i], dec=dec_s[rd, i], heads=[])
            for h in range(2):
                g = _dot_nt(xs_s[rd, 2 * i + h], ys)
                a = jnp.where(strict, g[:C, :C], 0.0)
                d["heads"].append(dict(
                    ab=a.astype(BF16),
                    t=eye - jnp.where(merge_masks[0], a, 0.0),
                    bm_rk=jnp.concatenate([jnp.where(strict, g[:C, C:], 0.0),
                                           jnp.where(incl, g[C:, C:], 0.0)], axis=0).astype(BF16),
                    rb=jnp.where(incl, g[C:, :C], 0.0).astype(BF16),
                ))
            pre[(p, c)] = d

    zero_b = jnp.zeros((C, C), BF16)

    def inverse_level_a(mm):
        for it in items:
            for ch in pre[it]["heads"]:
                ch["tb"] = ch["t"].astype(BF16)
                ch["m1"] = _dot(jnp.where(mm, ch["ab"], zero_b), ch["tb"]).astype(BF16)

    def inverse_level_b():
        for it in items:
            for ch in pre[it]["heads"]:
                ch["t"] = ch["t"] - _dot(ch["tb"], ch["m1"])

    def consume_apply():
        for it in items:
            d = pre[it]
            h0, h1 = d["heads"]
            d["x"] = _dot(jnp.concatenate([h0["bm_rk"], h1["bm_rk"]], axis=1), _head_stack(d["vb"], lane_lo))
        for it in items:
            d = pre[it]
            h0, h1 = d["heads"]
            t2 = jnp.concatenate([h0["t"], h1["t"]], axis=1).astype(BF16)
            rhs = jnp.concatenate([d["kap_st"], d["x"][:C].astype(BF16)], axis=1)
            d["tk"] = _dot(t2, _head_stack(rhs, lane_lo2))
        for it in items:
            d = pre[it]
            h0, h1 = d["heads"]
            d["w_st"] = d["tk"][:, :P].astype(BF16)
            d["rb2"] = jnp.concatenate([h0["rb"], h1["rb"]], axis=1)

    def consume_recurrence():
        inv = 1.0 / RW_HD
        pairs = range(RW_PAIRS)
        st = {p: st_s[p] for p in pairs}
        ys_out = {}
        for c in order:
            ws_rs, ub, upd = {}, {}, {}
            for p in pairs:
                d = pre[(p, c)]
                ws_rs[p] = _dot(jnp.concatenate([d["w_st"], d["r_st"]], axis=0), st[p].astype(BF16))
            for p in pairs:
                ub[p] = (-(ws_rs[p][:C] + pre[(p, c)]["tk"][:, P:])).astype(BF16)
            for p in pairs:
                d = pre[(p, c)]
                upd[p] = _dot(d["bk"], jnp.concatenate([ub[p], d["vb"]], axis=0))
            for p in pairs:
                st[p] = st[p] * pre[(p, c)]["dec"] + jnp.where(same_head, upd[p], 0.0)
            for p in pairs:
                d = pre[(p, c)]
                ys_out[(p, c)] = ws_rs[p][C:] + d["x"][C:] + _dot(d["rb2"], _head_stack(ub[p], lane_lo))
        for p in pairs:
            st_s[p] = st[p]
        if reverse:
            y = {p: yf_ref[0, p] + jnp.concatenate([ys_out[(p, c)] for c in range(n_chunks)], axis=0) for p in pairs}
            group_sum1 = lambda z: _dot(z.astype(BF16), ones_bd)
            dlt = {p: y[p] - group_sum1(y[p]) * inv for p in pairs}
            var = {p: group_sum1(dlt[p] * dlt[p]) * inv for p in pairs}
            bonus = {p: group_sum1(bon_s[rd, p]) * v_s[rd, p] for p in pairs}
            for p in pairs:
                sl = slice(p * P, (p + 1) * P)
                yn = dlt[p] * lax.rsqrt(var[p] + RW_LN_EPS) * lnw_ref[:, sl] + lnb_ref[:, sl]
                out_ref[0, :, sl] = ((yn + bonus[p]) * g_s[rd, p]).astype(out_ref.dtype)
        else:
            for p in pairs:
                for c in range(n_chunks):
                    out_ref[0, p, c * C:(c + 1) * C, :] = ys_out[(p, c)]

    def shifted(cols):
        m = mu_ref[:, cols]
        nbr = xpad_s[HALO - 1:HALO - 1 + TB, cols] + xpad_s[HALO + 1:HALO + 1 + TB, cols]
        return xpad_s[HALO:HALO + TB, cols] * (1.0 - m) + nbr * (0.5 * m)

    low = {}

    def prepare_common():
        seg_start = (blk == 0) | (blk == nc)
        seg_end = (blk == nc - 1) | (blk == nb - 1)
        xpad_s[0:HALO, :] = jnp.where(seg_start, 0.0, xp_ref[0])
        xpad_s[HALO:HALO + TB, :] = x_ref[0]
        xpad_s[HALO + TB:, :] = jnp.where(seg_end, 0.0, xn_ref[0])
        low["dd"] = jnp.tanh(shifted(slice(3 * D, 3 * D + RW_RANK))).astype(BF16)
        low["ad"] = shifted(slice(3 * D + RW_RANK, 3 * D + 2 * RW_RANK)).astype(BF16)
        if reverse:
            low["gd"] = jax.nn.sigmoid(shifted(slice(3 * D + 2 * RW_RANK, 3 * D + 3 * RW_RANK))).astype(BF16)

    def prepare_pair(p):
        sl = slice(p * P, (p + 1) * P)
        r = shifted(slice(p * P, (p + 1) * P))
        k = shifted(slice(D + p * P, D + (p + 1) * P))
        v = shifted(slice(2 * D + p * P, 2 * D + (p + 1) * P))
        logw = -RW_DECAY_SCALE * jax.nn.sigmoid(d0_ref[:, sl] + _dot(low["dd"], du_ref[:, sl]))
        a = jax.nn.sigmoid(a0_ref[:, sl] + _dot(low["ad"], au_ref[:, sl]))
        kt = k * (1.0 + (a - 1.0) * ka_ref[:, sl])
        if reverse:
            a_f = jax.nn.sigmoid(a0f_ref[:, sl] + _dot(low["ad"], auf_ref[:, sl]))
            bon_s[wr, p] = r * (kt + k * (1.0 + (a_f - 1.0) * ka_ref[:, sl])) * rk_ref[:, sl]
            g_s[wr, p] = _dot(low["gd"], gu_ref[:, sl])
            v_s[wr, p] = v
        kkp = k * kk_ref[:, sl]
        kap_p = kkp / jnp.maximum(jnp.sqrt(_group_sum(kkp * kkp, ones_bd)), 1e-12)
        b_p = kap_p * a
        for c in range(n_chunks):
            i = slot_of(p, c)
            rows = slice(c * C, (c + 1) * C)
            lw = logw[rows]
            rr = r[rows]
            kap = kap_p[rows]
            ktc = kt[rows]
            bb = b_p[rows]
            L = _tri_sum_left(tri, lw)
            Lx = L - lw
            Lm = L[mid:mid + 1, :]
            Lc = jnp.sum(lw, axis=0, keepdims=True)
            e_neg = jnp.exp(Lm - L)
            e_end = jnp.exp(Lc - L)
            xs = jnp.concatenate([kap * jnp.exp(Lx - Lm), rr * jnp.exp(L - Lm)], axis=0)
            for h, hm in enumerate(head_masks):
                xs_s[wr, 2 * i + h] = jnp.where(hm, xs, 0.0).astype(BF16)
            ys_s[wr, i] = jnp.concatenate([bb * e_neg, ktc * e_neg], axis=0).astype(BF16)
            op_s[wr, 3 * i] = (kap * jnp.exp(Lx)).astype(BF16)
            op_s[wr, 3 * i + 1] = (rr * jnp.exp(L)).astype(BF16)
            op_s[wr, 3 * i + 2] = v[rows].astype(BF16)
            bk_s[wr, i] = jnp.concatenate([(bb * e_end).T, (ktc * e_end).T], axis=1).astype(BF16)
            dec_s[wr, i] = jnp.exp(jnp.broadcast_to(Lc, (P, P)).T)

    prepare_common()
    consume_gram()
    todo = list(range(RW_PAIRS))
    for mm in merge_masks[1:]:
        inverse_level_a(mm)
        if todo:
            prepare_pair(todo.pop(0))
        inverse_level_b()
    consume_apply()
    if todo:
        prepare_pair(todo.pop(0))
    consume_recurrence()
    for p in todo:
        prepare_pair(p)


def _rwkv(x, mu, k_k, k_a, d0, du, a0, au, gate_up, r_k, ln_w, ln_b, nc):
    B, TA, WS = x.shape
    D = RW_HEADS * RW_HD
    P = 2 * RW_HD
    C = RW_CHUNK
    nb = TA // TOKEN_BLOCK
    n_items = RW_PAIRS * (TOKEN_BLOCK // C)
    const = lambda shape: pl.BlockSpec(shape, lambda b, j: (0,) * len(shape))
    row = lambda a: a.reshape(1, -1)
    slab = lambda: pltpu.VMEM((2, RW_PAIRS, TOKEN_BLOCK, P), F32)

    def run(d, extra_in, extra_specs, out_spec, out_shape, extra_scratch):
        reverse = d == 1
        blk = _block_fn(reverse, nc, nb)
        blk_in = lambda j: blk(jnp.minimum(j, nb - 1))
        blk_out = lambda j: blk(jnp.maximum(j - 1, 0))
        tok = pl.BlockSpec((1, TOKEN_BLOCK, WS), lambda b, j: (b, blk_in(j), 0))
        prev, nxt = _halo_specs(WS, blk_in, TA)
        in_specs = [tok, prev, nxt, const((1, WS)), const((1, D)), const((1, D)),
                    const((1, D)), const((RW_RANK, D)), const((1, D)), const((RW_RANK, D))] + extra_specs(blk_out)
        scratch = [
            pltpu.VMEM((TOKEN_BLOCK + 2 * HALO, WS), F32),
            pltpu.VMEM((2, 2 * n_items, 2 * C, P), BF16),
            pltpu.VMEM((2, n_items, 2 * C, P), BF16),
            pltpu.VMEM((2, 3 * n_items, C, P), BF16),
            pltpu.VMEM((2, n_items, P, 2 * C), BF16),
            pltpu.VMEM((2, n_items, P, P), F32),
        ] + extra_scratch + [pltpu.VMEM((RW_PAIRS, P, P), F32)]
        return pl.pallas_call(
            functools.partial(_rwkv_kernel, reverse, nc, nb),
            grid=(B, nb + 1),
            in_specs=in_specs,
            out_specs=out_spec(blk_out),
            out_shape=out_shape,
            scratch_shapes=scratch,
            compiler_params=_cparams("arbitrary", "arbitrary"),
        )(x, x, x, row(mu), row(k_k), row(k_a), d0[d:d + 1], du[d], a0[d:d + 1], au[d], *extra_in)

    pair_spec = lambda blk: pl.BlockSpec((1, RW_PAIRS, TOKEN_BLOCK, P), lambda b, j: (b, 0, blk(j), 0))
    yf = run(0, (), lambda blk: [], pair_spec, jax.ShapeDtypeStruct((B, RW_PAIRS, TA, P), F32), [])
    return run(
        1,
        (a0[0:1], au[0], gate_up, row(r_k), row(ln_w), row(ln_b), yf),
        lambda blk: [const((1, D)), const((RW_RANK, D)), const((RW_RANK, D)), const((1, D)), const((1, D)),
                     const((1, D)), pair_spec(blk)],
        lambda blk: pl.BlockSpec((1, TOKEN_BLOCK, D), lambda b, j: (b, blk(j), 0)),
        jax.ShapeDtypeStruct((B, TA, D), BF16),
        [slab(), slab(), slab()],
    )


def _merge_kernel(x_ref, yml_ref, ylru_ref, yrw_ref, g_ref, oml_ref, olru_ref, orw_ref, wout_ref,
                  mod_ref, nw_ref, xo_ref, h_ref):
    D = x_ref.shape[-1]
    g = g_ref[0]
    acc = jax.nn.sigmoid(g[:, 0:D]) * _dot(yml_ref[0], oml_ref[...])
    acc = acc + jax.nn.sigmoid(g[:, D:2 * D]) * _dot(ylru_ref[0], olru_ref[...])
    acc = acc + jax.nn.sigmoid(g[:, 2 * D:3 * D]) * _dot(yrw_ref[0], orw_ref[...])
    y = _dot(acc.astype(BF16), wout_ref[...])
    m = mod_ref[0]
    xn = x_ref[0] + m[2:3] * y
    xo_ref[0] = xn
    h_ref[0] = (_rms(xn, nw_ref[...]) * (1.0 + m[4:5]) + m[3:4]).astype(BF16)


def _merge(x, yml, ylru, yrw, gates, oml, olru, orw, wout, mod, norm2_w, nc, skip_ctx):
    B, TA, D = x.shape
    nb = TA // TOKEN_BLOCK
    off = nc if skip_ctx else 0
    rows_out = (nb - off) * TOKEN_BLOCK
    tok = lambda w: pl.BlockSpec((1, TOKEN_BLOCK, w), lambda b, j: (b, j + off, 0))
    otok = pl.BlockSpec((1, TOKEN_BLOCK, D), lambda b, j: (b, j, 0))
    wspec = pl.BlockSpec((D, D), lambda b, j: (0, 0))
    return pl.pallas_call(
        _merge_kernel,
        grid=(B, nb - off),
        in_specs=[tok(D), tok(D), tok(D), tok(D), tok(3 * D), wspec, wspec, wspec, wspec,
                  pl.BlockSpec((1, 6, D), lambda b, j: (jnp.where(j + off < nc, B, b), 0, 0)),
                  pl.BlockSpec((1, D), lambda b, j: (0, 0))],
        out_specs=[otok, otok],
        out_shape=[jax.ShapeDtypeStruct((B, rows_out, D), F32), jax.ShapeDtypeStruct((B, rows_out, D), BF16)],
        input_output_aliases={} if skip_ctx else {0: 0},
        compiler_params=_cparams("arbitrary", "arbitrary"),
    )(x, yml, ylru, yrw, gates, oml, olru, orw, wout, mod, norm2_w.reshape(1, D))


def _ffn_kernel(x_ref, h_ref, win_ref, wout_ref, mod_ref, xo_ref):
    F = wout_ref.shape[0]
    u = _dot(h_ref[0], win_ref[...])
    gate = u[:, :F]
    act = (gate * jax.nn.sigmoid(gate) * u[:, F:]).astype(BF16)
    xo_ref[0] = x_ref[0] + mod_ref[0][5:6] * _dot(act, wout_ref[...])


def _ffn(x, h, win, wout, mod, nc):
    B, TA, D = x.shape
    F = wout.shape[0]
    nb = TA // TOKEN_BLOCK
    tok = pl.BlockSpec((1, TOKEN_BLOCK, D), lambda b, j: (b, j, 0))
    return pl.pallas_call(
        _ffn_kernel,
        grid=(B, nb),
        in_specs=[tok, tok,
                  pl.BlockSpec((D, 2 * F), lambda b, j: (0, 0)),
                  pl.BlockSpec((F, D), lambda b, j: (0, 0)),
                  pl.BlockSpec((1, 6, D), lambda b, j: (jnp.where(j < nc, B, b), 0, 0))],
        out_specs=tok,
        out_shape=jax.ShapeDtypeStruct((B, TA, D), F32),
        input_output_aliases={0: 0},
        compiler_params=_cparams("arbitrary", "arbitrary"),
    )(x, h, win, wout, mod)


def _grid_transpose_latent(xa, ctx_len, rows, cols):
    B, TA, D = xa.shape
    lat = xa[:, ctx_len:].reshape(B, rows, cols, D).transpose(0, 2, 1, 3).reshape(B, rows * cols, D)
    return jnp.concatenate([xa[:, :ctx_len], lat], axis=1)


def _block_diag_groups(w):
    per = LRU_GROUP // LRU_BD
    d2 = w.shape[0]
    wg = w.reshape(d2, LRU_BLOCKS // per, per, LRU_BD, LRU_BD)
    eye = jnp.eye(per, dtype=w.dtype)
    out = jnp.einsum('dgpio,pq->dgpiqo', wg, eye)
    return out.reshape(d2, LRU_BLOCKS // per, LRU_GROUP, LRU_GROUP).astype(BF16)


def _pad_dir_rows(w):
    z = jnp.zeros_like(w[0])
    return jnp.stack([jnp.concatenate([w[0], z], axis=0), jnp.concatenate([z, w[1]], axis=0)]).astype(BF16)


def kernel(x, c, ctx, c_ctx, w_mod, b_mod, norm1_w, norm2_w, w_in, ml_ig_b, ml_fg_b, ml_norm_w, lru_conv_w, lru_conv_b, lru_gr_w, lru_gr_b, lru_gi_w, lru_gi_b, lru_lambda, rw_mu, rw_decay0, rw_decay_up, rw_iclr0, rw_iclr_up, rw_gate_up, rw_k_k, rw_k_a, rw_r_k, rw_ln_w, rw_ln_b, out_ml, out_lru, out_rw, w_out, w_ffn_in, w_ffn_out, final_norm_w):
    B, T, D = x.shape
    CTX = ctx.shape[1]
    L = w_mod.shape[0]
    TA = CTX + T
    assert CTX % TOKEN_BLOCK == 0 and T % TOKEN_BLOCK == 0 and T % GRID_W == 0
    nc = CTX // TOKEN_BLOCK
    rows = T // GRID_W
    N = B * TA
    MLW = ML_HEADS * ML_HD

    BP = -(-(B + 1) // 8) * 8
    cond = jnp.concatenate([c, c_ctx[None, :], jnp.zeros((BP - B - 1, D), F32)], axis=0)
    mods = _adaln_all(cond, w_mod, b_mod).reshape(L, BP, 6, D)

    xa = jnp.concatenate([ctx, x], axis=1)
    column_order = False
    for layer in range(L):
        last = layer == L - 1
        want_columns = layer % 2 == 1
        if want_columns != column_order:
            xa = _grid_transpose_latent(xa, CTX, GRID_W if column_order else rows, rows if column_order else GRID_W)
            column_order = want_columns
        mod = mods[layer]

        wi = w_in[layer]
        o = 0
        wq = wi[:, o:o + MLW]; o += MLW
        wk = wi[:, o:o + MLW] * (ML_HD ** -0.5); o += MLW
        wv = wi[:, o:o + MLW]; o += MLW
        wo = wi[:, o:o + MLW]; o += MLW
        wif = wi[:, o:o + 4 * ML_HEADS]; o += 4 * ML_HEADS
        wlx = wi[:, o:o + D]; o += D
        wlg = wi[:, o:o + D]; o += D
        rw_w = 3 * D + 3 * RW_RANK
        wrw = wi[:, o:o + rw_w]; o += rw_w
        wmg = wi[:, o:o + 3 * D]; o += 3 * D
        w_qkv = jnp.concatenate([wq, wk, wv], axis=1).astype(BF16)
        w_if = jnp.concatenate([wif, jnp.zeros((D, 128 - 4 * ML_HEADS), F32)], axis=1).astype(BF16)

        h = _prenorm(xa, norm1_w[layer], mod, nc).reshape(N, D)
        qkv = _matmul(h, w_qkv, BF16, 1024).reshape(B, TA, 3 * MLW)
        ml_o = _matmul(h, wo.astype(BF16), F32, 1024).reshape(B, TA, MLW)
        gcol = _matmul(h, w_if, F32, 128).reshape(B, TA, 128)
        lru_x = _matmul(h, wlx.astype(BF16), F32, 1024).reshape(B, TA, D)
        lru_g = _matmul(h, wlg.astype(BF16), F32, 1024).reshape(B, TA, D)
        rw = _matmul(h, wrw.astype(BF16), F32, rw_w // 3).reshape(B, TA, rw_w)
        mgates = _matmul(h, wmg.astype(BF16), F32, 1024).reshape(B, TA, 3 * D)

        grow = gcol[:, :, :16].reshape(B, TA // ML_CHUNK, ML_CHUNK, 16).transpose(0, 1, 3, 2)
        gate_b = jnp.concatenate([ml_ig_b[layer].reshape(-1), ml_fg_b[layer].reshape(-1)])
        bcol = jnp.concatenate([gate_b, jnp.zeros((128 - 16,), F32)]).reshape(1, 128)
        y_ml = _mlstm(qkv, gcol, grow, bcol, gate_b.reshape(16, 1), ml_o, ml_norm_w[layer], nc)

        y_lru = _lru(lru_x, lru_g, lru_conv_w[layer], lru_conv_b[layer],
                     _block_diag_groups(lru_gr_w[layer]), _block_diag_groups(lru_gi_w[layer]),
                     lru_gr_b[layer], lru_gi_b[layer], lru_lambda[layer], nc)

        y_rw = _rwkv(rw, rw_mu[layer], rw_k_k[layer], rw_k_a[layer], rw_decay0[layer],
                     _pad_dir_rows(rw_decay_up[layer]), rw_iclr0[layer], _pad_dir_rows(rw_iclr_up[layer]),
                     rw_gate_up[layer].astype(BF16), rw_r_k[layer], rw_ln_w[layer], rw_ln_b[layer], nc)

        xa, h2 = _merge(xa, y_ml, y_lru, y_rw, mgates, out_ml[layer].astype(BF16), out_lru[layer].astype(BF16),
                        out_rw[layer].astype(BF16), w_out[layer].astype(BF16), mod, norm2_w[layer], nc, last)
        xa = _ffn(xa, h2, w_ffn_in[layer].astype(BF16), w_ffn_out[layer].astype(BF16), mod, 0 if last else nc)

    out = _final_norm(xa, final_norm_w, 0)
    if column_order:
        out = out.reshape(B, GRID_W, rows, D).transpose(0, 2, 1, 3).reshape(B, T, D)
    return out
```

```python
import functools
import math

import jax
import jax.numpy as jnp
from jax import lax
from jax.experimental import pallas as pl
from jax.experimental.pallas import tpu as pltpu

F32 = jnp.float32
BF16 = jnp.bfloat16

NORM_EPS = 1e-6
GRID_W = 64
TOKEN_BLOCK = 256
HALO = 8

ML_HEADS = 4
ML_HD = 256
ML_CHUNK = 64
ML_M_INIT = -1e30

LRU_BLOCKS = 16
LRU_BD = 64
LRU_CONV = 4
LRU_C = 8.0
LRU_GROUP = 256
RW_HD = 64
RW_HEADS = 16
RW_PAIRS = 8
RW_CHUNK = 128
RW_RANK = 128
RW_DECAY_SCALE = math.exp(-0.5)
RW_LN_EPS = 64e-5

VMEM_LIMIT = 56 * 1024 * 1024


def _cparams(*sem):
    return pltpu.CompilerParams(dimension_semantics=sem, vmem_limit_bytes=VMEM_LIMIT)


def _dot(a, b):
    return jnp.dot(a, b, preferred_element_type=F32)


def _dot_nt(a, b):
    return lax.dot_general(a, b, (((1,), (1,)), ((), ())), preferred_element_type=F32)


def _split2(x):
    h = x.astype(BF16)
    return h, (x - h.astype(F32)).astype(BF16)


def _tri_sum_left(tri, x):
    w = x.shape[1]
    y = _dot(tri, jnp.concatenate(_split2(x), axis=1))
    return y[:, :w] + y[:, w:]


def _tri_sum_right(x, tri):
    r = x.shape[0]
    y = _dot(jnp.concatenate(_split2(x), axis=0), tri)
    return y[:r] + y[r:]


def _group_sum(x, ones_bd):
    n = x.shape[0]
    h = x.astype(BF16)
    l = (x - h.astype(F32)).astype(BF16)
    y = _dot(jnp.concatenate([h, l], axis=0), ones_bd)
    return y[:n] + y[n:]


def _head_stack(z, head0):
    zero = jnp.zeros_like(z)
    return jnp.concatenate([jnp.where(head0, z, zero), jnp.where(head0, zero, z)], axis=0)


def _bwd_block(j, nc, nb):
    return jnp.where(j < nc, nc - 1 - j, nb + nc - 1 - j)


def _block_fn(reverse, nc, nb):
    if reverse:
        return lambda j: _bwd_block(j, nc, nb)
    return lambda j: j


def _mod_kernel(c_ref, w_ref, b_ref, o_ref):
    c = c_ref[...]
    s = (c * jax.nn.sigmoid(c)).astype(BF16)
    o_ref[0] = _dot(s, w_ref[0].astype(BF16)) + b_ref[0]


def _adaln_all(cond, w_mod, b_mod):
    L, D, D6 = w_mod.shape
    BP = cond.shape[0]
    tn = 1536
    return pl.pallas_call(
        _mod_kernel,
        grid=(L, D6 // tn),
        in_specs=[
            pl.BlockSpec((BP, D), lambda l, n: (0, 0)),
            pl.BlockSpec((1, D, tn), lambda l, n: (l, 0, n)),
            pl.BlockSpec((1, 1, tn), lambda l, n: (l, 0, n)),
        ],
        out_specs=pl.BlockSpec((1, BP, tn), lambda l, n: (l, 0, n)),
        out_shape=jax.ShapeDtypeStruct((L, BP, D6), F32),
        compiler_params=_cparams("arbitrary", "arbitrary"),
    )(cond, w_mod, b_mod.reshape(L, 1, D6))


def _rms(x, w):
    return x * lax.rsqrt(jnp.mean(x * x, axis=-1, keepdims=True) + NORM_EPS) * w


def _prenorm_kernel(x_ref, w_ref, mod_ref, h_ref):
    m = mod_ref[0]
    y = _rms(x_ref[0], w_ref[...])
    h_ref[0] = (y * (1.0 + m[1:2]) + m[0:1]).astype(BF16)


def _prenorm(x, norm_w, mod, nc):
    B, TA, D = x.shape
    nb = TA // TOKEN_BLOCK
    return pl.pallas_call(
        _prenorm_kernel,
        grid=(B, nb),
        in_specs=[
            pl.BlockSpec((1, TOKEN_BLOCK, D), lambda b, j: (b, j, 0)),
            pl.BlockSpec((1, D), lambda b, j: (0, 0)),
            pl.BlockSpec((1, 6, D), lambda b, j: (jnp.where(j < nc, B, b), 0, 0)),
        ],
        out_specs=pl.BlockSpec((1, TOKEN_BLOCK, D), lambda b, j: (b, j, 0)),
        out_shape=jax.ShapeDtypeStruct((B, TA, D), BF16),
        compiler_params=_cparams("arbitrary", "arbitrary"),
    )(x, norm_w.reshape(1, D), mod)


def _final_norm_kernel(x_ref, w_ref, o_ref):
    o_ref[0] = _rms(x_ref[0], w_ref[...])


def _final_norm(x, w, nc):
    B, TA, D = x.shape
    nl = TA // TOKEN_BLOCK - nc
    return pl.pallas_call(
        _final_norm_kernel,
        grid=(B, nl),
        in_specs=[
            pl.BlockSpec((1, TOKEN_BLOCK, D), lambda b, j: (b, j + nc, 0)),
            pl.BlockSpec((1, D), lambda b, j: (0, 0)),
        ],
        out_specs=pl.BlockSpec((1, TOKEN_BLOCK, D), lambda b, j: (b, j, 0)),
        out_shape=jax.ShapeDtypeStruct((B, nl * TOKEN_BLOCK, D), F32),
        compiler_params=_cparams("arbitrary", "arbitrary"),
    )(x, w.reshape(1, D))


def _mm_kernel(h_ref, w_ref, o_ref):
    o_ref[...] = _dot(h_ref[...], w_ref[...]).astype(o_ref.dtype)


def _pick(n, cands):
    for c in cands:
        if n % c == 0:
            return c
    raise ValueError(f"no block size for {n}")


def _matmul(h, w, out_dtype, tn):
    N, K = h.shape
    M = w.shape[1]
    tm = _pick(N, (2048, 1024, 512, 256))
    return pl.pallas_call(
        _mm_kernel,
        grid=(M // tn, N // tm),
        in_specs=[
            pl.BlockSpec((tm, K), lambda n, m: (m, 0)),
            pl.BlockSpec((K, tn), lambda n, m: (0, n)),
        ],
        out_specs=pl.BlockSpec((tm, tn), lambda n, m: (m, n)),
        out_shape=jax.ShapeDtypeStruct((N, M), out_dtype),
        compiler_params=_cparams("arbitrary", "arbitrary"),
    )(h, w)


def _mlstm_kernel(reverse, *refs):
    if reverse:
        (q_ref, k_ref, v_ref, kt_ref, gc_ref, gr_ref, bc_ref, br_ref, yf_ref, o_ref, nw_ref,
         out_ref, ct_ref, n_ref, m_ref) = refs
    else:
        (q_ref, k_ref, v_ref, kt_ref, gc_ref, gr_ref, bc_ref, br_ref,
         out_ref, ct_ref, n_ref, m_ref) = refs
    C = ML_CHUNK
    j = pl.program_id(1)

    @pl.when(j == 0)
    def _():
        ct_ref[...] = jnp.zeros_like(ct_ref)
        n_ref[...] = jnp.zeros_like(n_ref)
        m_ref[...] = jnp.full_like(m_ref, ML_M_INIT)

    row = lax.broadcasted_iota(jnp.int32, (C, C), 0)
    col = lax.broadcasted_iota(jnp.int32, (C, C), 1)
    causal = (col >= row) if reverse else (col <= row)
    tri_l = causal.astype(BF16)
    tri_r = ((row >= col) if reverse else (row <= col)).astype(BF16)
    last = 0 if reverse else C - 1
    d = 1 if reverse else 0

    n_chunks = TOKEN_BLOCK // C
    order = range(n_chunks - 1, -1, -1) if reverse else range(n_chunks)
    heads = range(ML_HEADS)
    items = [(c, h) for c in order for h in heads]
    lanes = lambda h: slice(h * ML_HD, (h + 1) * ML_HD)
    rows = lambda c: slice(c * C, (c + 1) * C)

    g = {}
    for c in order:
        gc = gc_ref[0, rows(c), :] + bc_ref[...]
        gr = gr_ref[0, c] + br_ref[...]
        fc = _tri_sum_left(tri_l, jax.nn.log_sigmoid(gc))
        fr = _tri_sum_right(jax.nn.log_sigmoid(gr), tri_r)
        for h in heads:
            gi = d * ML_HEADS + h
            g[(c, h)] = dict(i_c=gc[:, gi:gi + 1], f_c=fc[:, 8 + gi:9 + gi],
                             i_r=gr[gi:gi + 1, :], f_r=fr[8 + gi:9 + gi, :])
    for it in items:
        e = g[it]
        e["f_end"] = e["f_c"][last:last + 1, :]
        logw = jnp.where(causal, e["f_c"] - e["f_r"] + e["i_r"], -jnp.inf)
        e["m_loc"] = jnp.max(logw, axis=1, keepdims=True)
        e["w_loc"] = jnp.exp(logw - e["m_loc"])
        log_end_r = e["f_end"] - e["f_r"] + e["i_r"]
        e["m_end"] = jnp.max(log_end_r, axis=1, keepdims=True)
        e["w_end"] = jnp.exp(e["f_end"] - e["f_c"] + e["i_c"] - e["m_end"])
    for c, h in items:
        e = g[(c, h)]
        e["q"] = q_ref[0, rows(c), lanes(h)]
        e["s"] = _dot_nt(e["q"], k_ref[0, rows(c), lanes(h)]) * e["w_loc"]
    for c, h in items:
        e = g[(c, h)]
        e["num_loc"] = _dot(e["s"].astype(BF16), v_ref[0, rows(c), lanes(h)])
        e["den_loc"] = jnp.sum(e["s"], axis=1, keepdims=True)
    for c, h in items:
        e = g[(c, h)]
        wv = (e["w_end"] * v_ref[0, rows(c), lanes(h)].astype(F32)).astype(BF16)
        pad = [jnp.zeros((C, ML_HD), BF16)] * n_chunks
        pad[c] = wv
        e["kv_loc"] = _dot(kt_ref[0, lanes(h), :], jnp.concatenate(pad, axis=0))
        e["nk_loc"] = jnp.sum(e["w_end"] * k_ref[0, rows(c), lanes(h)].astype(F32), axis=0, keepdims=True)

    ct = {h: ct_ref[h] for h in heads}
    nv = {h: n_ref[h][0:1, :] for h in heads}
    ms = {h: m_ref[h][0:1, 0:1] for h in heads}
    h_out = {}
    for c in order:
        qc = {h: _dot(g[(c, h)]["q"], ct[h].astype(BF16)) for h in heads}
        for h in heads:
            e = g[(c, h)]
            inter = e["f_c"] + ms[h]
            m_t = jnp.maximum(e["m_loc"], inter)
            a_loc = jnp.exp(e["m_loc"] - m_t)
            w_inter = jnp.exp(inter - m_t)
            qn = jnp.sum(e["q"].astype(F32) * nv[h], axis=1, keepdims=True)
            num = a_loc * e["num_loc"] + w_inter * qc[h]
            den = a_loc * e["den_loc"] + w_inter * qn
            h_out[(c, h)] = num / jnp.maximum(jnp.abs(den), jnp.exp(-m_t))
        for h in heads:
            e = g[(c, h)]
            m_new = jnp.maximum(e["f_end"] + ms[h], e["m_end"])
            scale = jnp.exp(e["m_end"] - m_new)
            decay = jnp.exp(e["f_end"] + ms[h] - m_new)
            ct[h] = decay * ct[h] + scale * e["kv_loc"]
            nv[h] = decay * nv[h] + scale * e["nk_loc"]
            ms[h] = m_new
    for h in heads:
        ct_ref[h] = ct[h]
        n_ref[h] = jnp.broadcast_to(nv[h], n_ref.shape[1:])
        m_ref[h] = jnp.broadcast_to(ms[h], m_ref.shape[1:])

    for c, h in items:
        if reverse:
            tot = yf_ref[0, rows(c), lanes(h)] + h_out[(c, h)]
            hn = tot * lax.rsqrt(jnp.mean(tot * tot, axis=-1, keepdims=True) + NORM_EPS) * nw_ref[:, lanes(h)]
            out_ref[0, rows(c), lanes(h)] = (hn * jax.nn.sigmoid(o_ref[0, rows(c), lanes(h)])).astype(out_ref.dtype)
        else:
            out_ref[0, rows(c), lanes(h)] = h_out[(c, h)]


def _mlstm(qkv, gcol, grow, bcol, brow, o, norm_w, nc):
    B, TA, W3 = qkv.shape
    W = W3 // 3
    nb = TA // TOKEN_BLOCK
    cpb = TOKEN_BLOCK // ML_CHUNK
    scratch = [
        pltpu.VMEM((ML_HEADS, ML_HD, ML_HD), F32),
        pltpu.VMEM((ML_HEADS, 8, ML_HD), F32),
        pltpu.VMEM((ML_HEADS, 8, 128), F32),
    ]
    kt = jnp.swapaxes(qkv[:, :, W:2 * W], 1, 2)

    def run(reverse, extra_in, extra_specs, out_dtype):
        blk = _block_fn(reverse, nc, nb)
        tok = lambda w, cb: pl.BlockSpec((1, TOKEN_BLOCK, w), lambda b, j: (b, blk(j), cb))
        in_specs = [
            tok(W, 0), tok(W, 1), tok(W, 2),
            pl.BlockSpec((1, W, TOKEN_BLOCK), lambda b, j: (b, 0, blk(j))),
            pl.BlockSpec((1, TOKEN_BLOCK, 128), lambda b, j: (b, blk(j), 0)),
            pl.BlockSpec((1, cpb, 16, ML_CHUNK), lambda b, j: (b, blk(j), 0, 0)),
            pl.BlockSpec((1, 128), lambda b, j: (0, 0)),
            pl.BlockSpec((16, 1), lambda b, j: (0, 0)),
        ] + extra_specs(tok)
        return pl.pallas_call(
            functools.partial(_mlstm_kernel, reverse),
            grid=(B, nb),
            in_specs=in_specs,
            out_specs=tok(W, 0),
            out_shape=jax.ShapeDtypeStruct((B, TA, W), out_dtype),
            scratch_shapes=scratch,
            compiler_params=_cparams("arbitrary", "arbitrary"),
        )(qkv, qkv, qkv, kt, gcol, grow, bcol, brow, *extra_in)

    yf = run(False, (), lambda tok: [], F32)
    return run(True, (yf, o, norm_w.reshape(1, W)),
               lambda tok: [tok(W, 0), tok(W, 0), pl.BlockSpec((1, W), lambda b, j: (0, 0))], BF16)


def _halo_specs(width, blk, TA):
    per = TOKEN_BLOCK // HALO
    last = TA // HALO - 1
    prev = pl.BlockSpec((1, HALO, width), lambda b, j: (b, jnp.maximum(blk(j) * per - 1, 0), 0))
    nxt = pl.BlockSpec((1, HALO, width), lambda b, j: (b, jnp.minimum((blk(j) + 1) * per, last), 0))
    return prev, nxt


def _lru_kernel(reverse, nc, nb, *refs):
    if reverse:
        (x_ref, xp_ref, xn_ref, cw_ref, cb_ref, wr_ref, wi_ref, br_ref, bi_ref, lam_ref,
         hf_ref, gate_ref, out_ref, xpad_s, a_s, b_s, y_s, h_s) = refs
    else:
        (x_ref, xp_ref, xn_ref, cw_ref, cb_ref, wr_ref, wi_ref, br_ref, bi_ref, lam_ref,
         out_ref, xpad_s, a_s, b_s, y_s, h_s) = refs
    TB = TOKEN_BLOCK
    W = x_ref.shape[-1]
    n_groups = W // LRU_GROUP
    j = pl.program_id(1)
    wr = lax.rem(j, 2)
    rd = 1 - wr
    jin = jnp.minimum(j, nb - 1)
    blk = _bwd_block(jin, nc, nb) if reverse else jin

    @pl.when(j == 0)
    def _():
        h_s[...] = jnp.zeros_like(h_s)
        a_s[1] = jnp.zeros(a_s.shape[1:], F32)
        b_s[1] = jnp.zeros(b_s.shape[1:], F32)

    seg_start = (blk == 0) | (blk == nc)
    seg_end = (blk == nc - 1) | (blk == nb - 1)
    xpad_s[0:HALO, :] = jnp.where(seg_start, 0.0, xp_ref[0])
    xpad_s[HALO:HALO + TB, :] = x_ref[0]
    xpad_s[HALO + TB:, :] = jnp.where(seg_end, 0.0, xn_ref[0])
    left = LRU_CONV // 2

    def prepare_group(g):
        cols = slice(g * LRU_GROUP, (g + 1) * LRU_GROUP)
        u = cb_ref[:, cols]
        for tap in range(LRU_CONV):
            off = HALO - left + tap
            u = u + cw_ref[tap:tap + 1, cols] * xpad_s[off:off + TB, cols]
        ug = u.astype(BF16)
        r = jax.nn.sigmoid(_dot(ug, wr_ref[g]) + br_ref[:, cols])
        i = jax.nn.sigmoid(_dot(ug, wi_ref[g]) + bi_ref[:, cols])
        a = jnp.exp(-LRU_C * r * jax.nn.softplus(-lam_ref[:, cols]))
        a_s[wr, :, cols] = a
        b_s[wr, :, cols] = jnp.sqrt(jnp.maximum(1.0 - a * a, 0.0)) * (i * u)

    def scan_rows(lo, hi, h):
        for t in range(lo, hi):
            idx = (TB - 1 - t) if reverse else t
            h = a_s[rd, idx:idx + 1, :] * h + b_s[rd, idx:idx + 1, :]
            y_s[idx:idx + 1, :] = h
        return h

    h = h_s[...]
    per = TB // n_groups
    for g in range(n_groups):
        h = scan_rows(g * per, (g + 1) * per, h)
        prepare_group(g)
    h_s[...] = h
    if reverse:
        out_ref[0] = ((hf_ref[0] + y_s[...]) * jax.nn.gelu(gate_ref[0])).astype(out_ref.dtype)
    else:
        out_ref[0] = y_s[...]


def _lru(x, gate, cw, cb, wr, wi, br, bi, lam, nc):
    B, TA, W = x.shape
    nb = TA // TOKEN_BLOCK
    ng = W // LRU_GROUP
    scratch = [
        pltpu.VMEM((TOKEN_BLOCK + 2 * HALO, W), F32),
        pltpu.VMEM((2, TOKEN_BLOCK, W), F32),
        pltpu.VMEM((2, TOKEN_BLOCK, W), F32),
        pltpu.VMEM((TOKEN_BLOCK, W), F32),
        pltpu.VMEM((1, W), F32),
    ]

    def run(d, extra_in, out_dtype):
        reverse = d == 1
        blk = _block_fn(reverse, nc, nb)
        blk_in = lambda j: blk(jnp.minimum(j, nb - 1))
        blk_out = lambda j: blk(jnp.maximum(j - 1, 0))
        tok = pl.BlockSpec((1, TOKEN_BLOCK, W), lambda b, j: (b, blk_in(j), 0))
        otok = pl.BlockSpec((1, TOKEN_BLOCK, W), lambda b, j: (b, blk_out(j), 0))
        prev, nxt = _halo_specs(W, blk_in, TA)
        const = lambda shape: pl.BlockSpec(shape, lambda b, j: (0,) * len(shape))
        in_specs = [tok, prev, nxt, const((LRU_CONV, W)), const((1, W)),
                    const((ng, LRU_GROUP, LRU_GROUP)), const((ng, LRU_GROUP, LRU_GROUP)),
                    const((1, W)), const((1, W)), const((1, W))] + [otok] * len(extra_in)
        return pl.pallas_call(
            functools.partial(_lru_kernel, reverse, nc, nb),
            grid=(B, nb + 1),
            in_specs=in_specs,
            out_specs=otok,
            out_shape=jax.ShapeDtypeStruct((B, TA, W), out_dtype),
            scratch_shapes=scratch,
            compiler_params=_cparams("arbitrary", "arbitrary"),
        )(x, x, x, cw, cb.reshape(1, W), wr[d], wi[d], br[d:d + 1], bi[d:d + 1], lam[d:d + 1], *extra_in)

    hf = run(0, (), F32)
    return run(1, (hf, gate), BF16)


def _rwkv_kernel(reverse, nc, nb, *refs):
    if reverse:
        (x_ref, xp_ref, xn_ref, mu_ref, kk_ref, ka_ref, d0_ref, du_ref, a0_ref, au_ref,
         a0f_ref, auf_ref, gu_ref, rk_ref, lnw_ref, lnb_ref, yf_ref,
         out_ref, xpad_s, xs_s, ys_s, op_s, bk_s, dec_s, v_s, bon_s, g_s, st_s) = refs
        handoff = (xs_s, ys_s, op_s, bk_s, dec_s, v_s, bon_s, g_s)
    else:
        (x_ref, xp_ref, xn_ref, mu_ref, kk_ref, ka_ref, d0_ref, du_ref, a0_ref, au_ref,
         out_ref, xpad_s, xs_s, ys_s, op_s, bk_s, dec_s, st_s) = refs
        handoff = (xs_s, ys_s, op_s, bk_s, dec_s)
    TB = TOKEN_BLOCK
    C = RW_CHUNK
    P = 2 * RW_HD
    D = RW_HEADS * RW_HD
    n_chunks = TB // C
    j = pl.program_id(1)
    wr = lax.rem(j, 2)
    rd = 1 - wr
    jin = jnp.minimum(j, nb - 1)
    blk = _bwd_block(jin, nc, nb) if reverse else jin

    @pl.when(j == 0)
    def _():
        st_s[...] = jnp.zeros_like(st_s)
        for ref in handoff:
            ref[1] = jnp.zeros(ref.shape[1:], ref.dtype)

    lane = lax.broadcasted_iota(jnp.int32, (P, P), 1)
    sub = lax.broadcasted_iota(jnp.int32, (P, P), 0)
    same_head = (lane < RW_HD) == (sub < RW_HD)
    ones_bd = same_head.astype(BF16)
    row = lax.broadcasted_iota(jnp.int32, (C, C), 0)
    col = lax.broadcasted_iota(jnp.int32, (C, C), 1)
    incl = (col >= row) if reverse else (col <= row)
    strict = (col > row) if reverse else (col < row)
    tri = incl.astype(BF16)
    eye = (row == col).astype(F32)
    merge_masks = []
    for i in range(int(math.log2(C))):
        same_parent = jnp.right_shift(row, i + 1) == jnp.right_shift(col, i + 1)
        merge_masks.append(same_parent & (jnp.right_shift(row, i) != jnp.right_shift(col, i)))
    lane_lo = lax.broadcasted_iota(jnp.int32, (1, P), 1) < RW_HD
    head_masks = (lane_lo, jnp.logical_not(lane_lo))
    lane2 = lax.broadcasted_iota(jnp.int32, (1, 2 * P), 1)
    lane_lo2 = jnp.bitwise_and(lane2, P - 1) < RW_HD
    mid = C // 2
    order = range(n_chunks - 1, -1, -1) if reverse else range(n_chunks)
    items = [(p, c) for p in range(RW_PAIRS) for c in range(n_chunks)]
    slot_of = lambda p, c: p * n_chunks + c

    pre = {}

    def consume_gram():
        for p, c in items:
            i = slot_of(p, c)
            ys = ys_s[rd, i]
            d = dict(kap_st=op_s[rd, 3 * i], r_st=op_s[rd, 3 * i + 1], vb=op_s[rd, 3 * i + 2],
                     bk=bk_s[rd, i], dec=dec_s[rd, i], heads=[])
            for h in range(2):
                g = _dot_nt(xs_s[rd, 2 * i + h], ys)
                a = jnp.where(strict, g[:C, :C], 0.0)
                d["heads"].append(dict(
                    ab=a.astype(BF16),
                    t=eye - jnp.where(merge_masks[0], a, 0.0),
                    bm_rk=jnp.concatenate([jnp.where(strict, g[:C, C:], 0.0),
                                           jnp.where(incl, g[C:, C:], 0.0)], axis=0).astype(BF16),
                    rb=jnp.where(incl, g[C:, :C], 0.0).astype(BF16),
                ))
            pre[(p, c)] = d

    zero_b = jnp.zeros((C, C), BF16)

    def inverse_level_a(mm):
        for it in items:
            for ch in pre[it]["heads"]:
                ch["tb"] = ch["t"].astype(BF16)
                ch["m1"] = _dot(jnp.where(mm, ch["ab"], zero_b), ch["tb"]).astype(BF16)

    def inverse_level_b():
        for it in items:
            for ch in pre[it]["heads"]:
                ch["t"] = ch["t"] - _dot(ch["tb"], ch["m1"])

    def consume_apply():
        for it in items:
            d = pre[it]
            h0, h1 = d["heads"]
            d["x"] = _dot(jnp.concatenate([h0["bm_rk"], h1["bm_rk"]], axis=1), _head_stack(d["vb"], lane_lo))
        for it in items:
            d = pre[it]
            h0, h1 = d["heads"]
            t2 = jnp.concatenate([h0["t"], h1["t"]], axis=1).astype(BF16)
            rhs = jnp.concatenate([d["kap_st"], d["x"][:C].astype(BF16)], axis=1)
            d["tk"] = _dot(t2, _head_stack(rhs, lane_lo2))
        for it in items:
            d = pre[it]
            h0, h1 = d["heads"]
            d["w_st"] = d["tk"][:, :P].astype(BF16)
            d["rb2"] = jnp.concatenate([h0["rb"], h1["rb"]], axis=1)

    def consume_recurrence():
        inv = 1.0 / RW_HD
        pairs = range(RW_PAIRS)
        st = {p: st_s[p] for p in pairs}
        ys_out = {}
        for c in order:
            ws_rs, ub, upd = {}, {}, {}
            for p in pairs:
                d = pre[(p, c)]
                ws_rs[p] = _dot(jnp.concatenate([d["w_st"], d["r_st"]], axis=0), st[p].astype(BF16))
            for p in pairs:
                ub[p] = (-(ws_rs[p][:C] + pre[(p, c)]["tk"][:, P:])).astype(BF16)
            for p in pairs:
                d = pre[(p, c)]
                upd[p] = _dot(d["bk"], jnp.concatenate([ub[p], d["vb"]], axis=0))
            for p in pairs:
                st[p] = st[p] * pre[(p, c)]["dec"] + jnp.where(same_head, upd[p], 0.0)
            for p in pairs:
                d = pre[(p, c)]
                ys_out[(p, c)] = ws_rs[p][C:] + d["x"][C:] + _dot(d["rb2"], _head_stack(ub[p], lane_lo))
        for p in pairs:
            st_s[p] = st[p]
        if reverse:
            y = {p: yf_ref[0, p] + jnp.concatenate([ys_out[(p, c)] for c in range(n_chunks)], axis=0) for p in pairs}
            group_sum1 = lambda z: _dot(z.astype(BF16), ones_bd)
            dlt = {p: y[p] - group_sum1(y[p]) * inv for p in pairs}
            var = {p: group_sum1(dlt[p] * dlt[p]) * inv for p in pairs}
            bonus = {p: group_sum1(bon_s[rd, p]) * v_s[rd, p] for p in pairs}
            for p in pairs:
                sl = slice(p * P, (p + 1) * P)
                yn = dlt[p] * lax.rsqrt(var[p] + RW_LN_EPS) * lnw_ref[:, sl] + lnb_ref[:, sl]
                out_ref[0, :, sl] = ((yn + bonus[p]) * g_s[rd, p]).astype(out_ref.dtype)
        else:
            for p in pairs:
                for c in range(n_chunks):
                    out_ref[0, p, c * C:(c + 1) * C, :] = ys_out[(p, c)]

    def shifted(cols):
        m = mu_ref[:, cols]
        nbr = xpad_s[HALO - 1:HALO - 1 + TB, cols] + xpad_s[HALO + 1:HALO + 1 + TB, cols]
        return xpad_s[HALO:HALO + TB, cols] * (1.0 - m) + nbr * (0.5 * m)

    low = {}

    def prepare_common():
        seg_start = (blk == 0) | (blk == nc)
        seg_end = (blk == nc - 1) | (blk == nb - 1)
        xpad_s[0:HALO, :] = jnp.where(seg_start, 0.0, xp_ref[0])
        xpad_s[HALO:HALO + TB, :] = x_ref[0]
        xpad_s[HALO + TB:, :] = jnp.where(seg_end, 0.0, xn_ref[0])
        low["dd"] = jnp.tanh(shifted(slice(3 * D, 3 * D + RW_RANK))).astype(BF16)
        low["ad"] = shifted(slice(3 * D + RW_RANK, 3 * D + 2 * RW_RANK)).astype(BF16)
        if reverse:
            low["gd"] = jax.nn.sigmoid(shifted(slice(3 * D + 2 * RW_RANK, 3 * D + 3 * RW_RANK))).astype(BF16)

    def prepare_pair(p):
        sl = slice(p * P, (p + 1) * P)
        r = shifted(slice(p * P, (p + 1) * P))
        k = shifted(slice(D + p * P, D + (p + 1) * P))
        v = shifted(slice(2 * D + p * P, 2 * D + (p + 1) * P))
        logw = -RW_DECAY_SCALE * jax.nn.sigmoid(d0_ref[:, sl] + _dot(low["dd"], du_ref[:, sl]))
        a = jax.nn.sigmoid(a0_ref[:, sl] + _dot(low["ad"], au_ref[:, sl]))
        kt = k * (1.0 + (a - 1.0) * ka_ref[:, sl])
        if reverse:
            a_f = jax.nn.sigmoid(a0f_ref[:, sl] + _dot(low["ad"], auf_ref[:, sl]))
            bon_s[wr, p] = r * (kt + k * (1.0 + (a_f - 1.0) * ka_ref[:, sl])) * rk_ref[:, sl]
            g_s[wr, p] = _dot(low["gd"], gu_ref[:, sl])
            v_s[wr, p] = v
        kkp = k * kk_ref[:, sl]
        kap_p = kkp / jnp.maximum(jnp.sqrt(_group_sum(kkp * kkp, ones_bd)), 1e-12)
        b_p = kap_p * a
        for c in range(n_chunks):
            i = slot_of(p, c)
            rows = slice(c * C, (c + 1) * C)
            lw = logw[rows]
            rr = r[rows]
            kap = kap_p[rows]
            ktc = kt[rows]
            bb = b_p[rows]
            L = _tri_sum_left(tri, lw)
            Lx = L - lw
            Lm = L[mid:mid + 1, :]
            Lc = jnp.sum(lw, axis=0, keepdims=True)
            e_neg = jnp.exp(Lm - L)
            e_end = jnp.exp(Lc - L)
            xs = jnp.concatenate([kap * jnp.exp(Lx - Lm), rr * jnp.exp(L - Lm)], axis=0)
            for h, hm in enumerate(head_masks):
                xs_s[wr, 2 * i + h] = jnp.where(hm, xs, 0.0).astype(BF16)
            ys_s[wr, i] = jnp.concatenate([bb * e_neg, ktc * e_neg], axis=0).astype(BF16)
            op_s[wr, 3 * i] = (kap * jnp.exp(Lx)).astype(BF16)
            op_s[wr, 3 * i + 1] = (rr * jnp.exp(L)).astype(BF16)
            op_s[wr, 3 * i + 2] = v[rows].astype(BF16)
            bk_s[wr, i] = jnp.concatenate([(bb * e_end).T, (ktc * e_end).T], axis=1).astype(BF16)
            dec_s[wr, i] = jnp.exp(jnp.broadcast_to(Lc, (P, P)).T)

    prepare_common()
    consume_gram()
    todo = list(range(RW_PAIRS))
    for mm in merge_masks[1:]:
        inverse_level_a(mm)
        if todo:
            prepare_pair(todo.pop(0))
        inverse_level_b()
    consume_apply()
    if todo:
        prepare_pair(todo.pop(0))
    consume_recurrence()
    for p in todo:
        prepare_pair(p)


def _rwkv(x, mu, k_k, k_a, d0, du, a0, au, gate_up, r_k, ln_w, ln_b, nc):
    B, TA, WS = x.shape
    D = RW_HEADS * RW_HD
    P = 2 * RW_HD
    C = RW_CHUNK
    nb = TA // TOKEN_BLOCK
    n_items = RW_PAIRS * (TOKEN_BLOCK // C)
    const = lambda shape: pl.BlockSpec(shape, lambda b, j: (0,) * len(shape))
    row = lambda a: a.reshape(1, -1)
    slab = lambda: pltpu.VMEM((2, RW_PAIRS, TOKEN_BLOCK, P), F32)

    def run(d, extra_in, extra_specs, out_spec, out_shape, extra_scratch):
        reverse = d == 1
        blk = _block_fn(reverse, nc, nb)
        blk_in = lambda j: blk(jnp.minimum(j, nb - 1))
        blk_out = lambda j: blk(jnp.maximum(j - 1, 0))
        tok = pl.BlockSpec((1, TOKEN_BLOCK, WS), lambda b, j: (b, blk_in(j), 0))
        prev, nxt = _halo_specs(WS, blk_in, TA)
        in_specs = [tok, prev, nxt, const((1, WS)), const((1, D)), const((1, D)),
                    const((1, D)), const((RW_RANK, D)), const((1, D)), const((RW_RANK, D))] + extra_specs(blk_out)
        scratch = [
            pltpu.VMEM((TOKEN_BLOCK + 2 * HALO, WS), F32),
            pltpu.VMEM((2, 2 * n_items, 2 * C, P), BF16),
            pltpu.VMEM((2, n_items, 2 * C, P), BF16),
            pltpu.VMEM((2, 3 * n_items, C, P), BF16),
            pltpu.VMEM((2, n_items, P, 2 * C), BF16),
            pltpu.VMEM((2, n_items, P, P), F32),
        ] + extra_scratch + [pltpu.VMEM((RW_PAIRS, P, P), F32)]
        return pl.pallas_call(
            functools.partial(_rwkv_kernel, reverse, nc, nb),
            grid=(B, nb + 1),
            in_specs=in_specs,
            out_specs=out_spec(blk_out),
            out_shape=out_shape,
            scratch_shapes=scratch,
            compiler_params=_cparams("arbitrary", "arbitrary"),
        )(x, x, x, row(mu), row(k_k), row(k_a), d0[d:d + 1], du[d], a0[d:d + 1], au[d], *extra_in)

    pair_spec = lambda blk: pl.BlockSpec((1, RW_PAIRS, TOKEN_BLOCK, P), lambda b, j: (b, 0, blk(j), 0))
    yf = run(0, (), lambda blk: [], pair_spec, jax.ShapeDtypeStruct((B, RW_PAIRS, TA, P), F32), [])
    return run(
        1,
        (a0[0:1], au[0], gate_up, row(r_k), row(ln_w), row(ln_b), yf),
        lambda blk: [const((1, D)), const((RW_RANK, D)), const((RW_RANK, D)), const((1, D)), const((1, D)),
                     const((1, D)), pair_spec(blk)],
        lambda blk: pl.BlockSpec((1, TOKEN_BLOCK, D), lambda b, j: (b, blk(j), 0)),
        jax.ShapeDtypeStruct((B, TA, D), BF16),
        [slab(), slab(), slab()],
    )


def _merge_kernel(x_ref, yml_ref, ylru_ref, yrw_ref, g_ref, oml_ref, olru_ref, orw_ref, wout_ref,
                  mod_ref, nw_ref, xo_ref, h_ref):
    D = x_ref.shape[-1]
    g = g_ref[0]
    acc = jax.nn.sigmoid(g[:, 0:D]) * _dot(yml_ref[0], oml_ref[...])
    acc = acc + jax.nn.sigmoid(g[:, D:2 * D]) * _dot(ylru_ref[0], olru_ref[...])
    acc = acc + jax.nn.sigmoid(g[:, 2 * D:3 * D]) * _dot(yrw_ref[0], orw_ref[...])
    y = _dot(acc.astype(BF16), wout_ref[...])
    m = mod_ref[0]
    xn = x_ref[0] + m[2:3] * y
    xo_ref[0] = xn
    h_ref[0] = (_rms(xn, nw_ref[...]) * (1.0 + m[4:5]) + m[3:4]).astype(BF16)


def _merge(x, yml, ylru, yrw, gates, oml, olru, orw, wout, mod, norm2_w, nc, skip_ctx):
    B, TA, D = x.shape
    nb = TA // TOKEN_BLOCK
    off = nc if skip_ctx else 0
    rows_out = (nb - off) * TOKEN_BLOCK
    tok = lambda w: pl.BlockSpec((1, TOKEN_BLOCK, w), lambda b, j: (b, j + off, 0))
    otok = pl.BlockSpec((1, TOKEN_BLOCK, D), lambda b, j: (b, j, 0))
    wspec = pl.BlockSpec((D, D), lambda b, j: (0, 0))
    return pl.pallas_call(
        _merge_kernel,
        grid=(B, nb - off),
        in_specs=[tok(D), tok(D), tok(D), tok(D), tok(3 * D), wspec, wspec, wspec, wspec,
                  pl.BlockSpec((1, 6, D), lambda b, j: (jnp.where(j + off < nc, B, b), 0, 0)),
                  pl.BlockSpec((1, D), lambda b, j: (0, 0))],
        out_specs=[otok, otok],
        out_shape=[jax.ShapeDtypeStruct((B, rows_out, D), F32), jax.ShapeDtypeStruct((B, rows_out, D), BF16)],
        input_output_aliases={} if skip_ctx else {0: 0},
        compiler_params=_cparams("arbitrary", "arbitrary"),
    )(x, yml, ylru, yrw, gates, oml, olru, orw, wout, mod, norm2_w.reshape(1, D))


def _ffn_kernel(x_ref, h_ref, win_ref, wout_ref, mod_ref, xo_ref):
    F = wout_ref.shape[0]
    u = _dot(h_ref[0], win_ref[...])
    gate = u[:, :F]
    act = (gate * jax.nn.sigmoid(gate) * u[:, F:]).astype(BF16)
    xo_ref[0] = x_ref[0] + mod_ref[0][5:6] * _dot(act, wout_ref[...])


def _ffn(x, h, win, wout, mod, nc):
    B, TA, D = x.shape
    F = wout.shape[0]
    nb = TA // TOKEN_BLOCK
    tok = pl.BlockSpec((1, TOKEN_BLOCK, D), lambda b, j: (b, j, 0))
    return pl.pallas_call(
        _ffn_kernel,
        grid=(B, nb),
        in_specs=[tok, tok,
                  pl.BlockSpec((D, 2 * F), lambda b, j: (0, 0)),
                  pl.BlockSpec((F, D), lambda b, j: (0, 0)),
                  pl.BlockSpec((1, 6, D), lambda b, j: (jnp.where(j < nc, B, b), 0, 0))],
        out_specs=tok,
        out_shape=jax.ShapeDtypeStruct((B, TA, D), F32),
        input_output_aliases={0: 0},
        compiler_params=_cparams("arbitrary", "arbitrary"),
    )(x, h, win, wout, mod)


def _grid_transpose_latent(xa, ctx_len, rows, cols):
    B, TA, D = xa.shape
    lat = xa[:, ctx_len:].reshape(B, rows, cols, D).transpose(0, 2, 1, 3).reshape(B, rows * cols, D)
    return jnp.concatenate([xa[:, :ctx_len], lat], axis=1)


def _block_diag_groups(w):
    per = LRU_GROUP // LRU_BD
    d2 = w.shape[0]
    wg = w.reshape(d2, LRU_BLOCKS // per, per, LRU_BD, LRU_BD)
    eye = jnp.eye(per, dtype=w.dtype)
    out = jnp.einsum('dgpio,pq->dgpiqo', wg, eye)
    return out.reshape(d2, LRU_BLOCKS // per, LRU_GROUP, LRU_GROUP).astype(BF16)


def _pad_dir_rows(w):
    z = jnp.zeros_like(w[0])
    return jnp.stack([jnp.concatenate([w[0], z], axis=0), jnp.concatenate([z, w[1]], axis=0)]).astype(BF16)


def kernel(x, c, ctx, c_ctx, w_mod, b_mod, norm1_w, norm2_w, w_in, ml_ig_b, ml_fg_b, ml_norm_w, lru_conv_w, lru_conv_b, lru_gr_w, lru_gr_b, lru_gi_w, lru_gi_b, lru_lambda, rw_mu, rw_decay0, rw_decay_up, rw_iclr0, rw_iclr_up, rw_gate_up, rw_k_k, rw_k_a, rw_r_k, rw_ln_w, rw_ln_b, out_ml, out_lru, out_rw, w_out, w_ffn_in, w_ffn_out, final_norm_w):
    B, T, D = x.shape
    CTX = ctx.shape[1]
    L = w_mod.shape[0]
    TA = CTX + T
    assert CTX % TOKEN_BLOCK == 0 and T % TOKEN_BLOCK == 0 and T % GRID_W == 0
    nc = CTX // TOKEN_BLOCK
    rows = T // GRID_W
    N = B * TA
    MLW = ML_HEADS * ML_HD

    BP = -(-(B + 1) // 8) * 8
    cond = jnp.concatenate([c, c_ctx[None, :], jnp.zeros((BP - B - 1, D), F32)], axis=0)
    mods = _adaln_all(cond, w_mod, b_mod).reshape(L, BP, 6, D)

    xa = jnp.concatenate([ctx, x], axis=1)
    column_order = False
    for layer in range(L):
        last = layer == L - 1
        want_columns = layer % 2 == 1
        if want_columns != column_order:
            xa = _grid_transpose_latent(xa, CTX, GRID_W if column_order else rows, rows if column_order else GRID_W)
            column_order = want_columns
        mod = mods[layer]

        wi = w_in[layer]
        o = 0
        wq = wi[:, o:o + MLW]; o += MLW
        wk = wi[:, o:o + MLW] * (ML_HD ** -0.5); o += MLW
        wv = wi[:, o:o + MLW]; o += MLW
        wo = wi[:, o:o + MLW]; o += MLW
        wif = wi[:, o:o + 4 * ML_HEADS]; o += 4 * ML_HEADS
        wlx = wi[:, o:o + D]; o += D
        wlg = wi[:, o:o + D]; o += D
        rw_w = 3 * D + 3 * RW_RANK
        wrw = wi[:, o:o + rw_w]; o += rw_w
        wmg = wi[:, o:o + 3 * D]; o += 3 * D
        w_qkv = jnp.concatenate([wq, wk, wv], axis=1).astype(BF16)
        w_if = jnp.concatenate([wif, jnp.zeros((D, 128 - 4 * ML_HEADS), F32)], axis=1).astype(BF16)

        h = _prenorm(xa, norm1_w[layer], mod, nc).reshape(N, D)
        qkv = _matmul(h, w_qkv, BF16, 1024).reshape(B, TA, 3 * MLW)
        ml_o = _matmul(h, wo.astype(BF16), F32, 1024).reshape(B, TA, MLW)
        gcol = _matmul(h, w_if, F32, 128).reshape(B, TA, 128)
        lru_x = _matmul(h, wlx.astype(BF16), F32, 1024).reshape(B, TA, D)
        lru_g = _matmul(h, wlg.astype(BF16), F32, 1024).reshape(B, TA, D)
        rw = _matmul(h, wrw.astype(BF16), F32, rw_w // 3).reshape(B, TA, rw_w)
        mgates = _matmul(h, wmg.astype(BF16), F32, 1024).reshape(B, TA, 3 * D)

        grow = gcol[:, :, :16].reshape(B, TA // ML_CHUNK, ML_CHUNK, 16).transpose(0, 1, 3, 2)
        gate_b = jnp.concatenate([ml_ig_b[layer].reshape(-1), ml_fg_b[layer].reshape(-1)])
        bcol = jnp.concatenate([gate_b, jnp.zeros((128 - 16,), F32)]).reshape(1, 128)
        y_ml = _mlstm(qkv, gcol, grow, bcol, gate_b.reshape(16, 1), ml_o, ml_norm_w[layer], nc)

        y_lru = _lru(lru_x, lru_g, lru_conv_w[layer], lru_conv_b[layer],
                     _block_diag_groups(lru_gr_w[layer]), _block_diag_groups(lru_gi_w[layer]),
                     lru_gr_b[layer], lru_gi_b[layer], lru_lambda[layer], nc)

        y_rw = _rwkv(rw, rw_mu[layer], rw_k_k[layer], rw_k_a[layer], rw_decay0[layer],
                     _pad_dir_rows(rw_decay_up[layer]), rw_iclr0[layer], _pad_dir_rows(rw_iclr_up[layer]),
                     rw_gate_up[layer].astype(BF16), rw_r_k[layer], rw_ln_w[layer], rw_ln_b[layer], nc)

        xa, h2 = _merge(xa, y_ml, y_lru, y_rw, mgates, out_ml[layer].astype(BF16), out_lru[layer].astype(BF16),
                        out_rw[layer].astype(BF16), w_out[layer].astype(BF16), mod, norm2_w[layer], nc, last)
        xa = _ffn(xa, h2, w_ffn_in[layer].astype(BF16), w_ffn_out[layer].astype(BF16), mod, 0 if last else nc)

    out = _final_norm(xa, final_norm_w, 0)
    if column_order:
        out = out.reshape(B, GRID_W, rows, D).transpose(0, 2, 1, 3).reshape(B, T, D)
    return out
```

```python
import functools
import math

import jax
import jax.numpy as jnp
from jax import lax
from jax.experimental import pallas as pl
from jax.experimental.pallas import tpu as pltpu

F32 = jnp.float32
BF16 = jnp.bfloat16

NORM_EPS = 1e-6
GRID_W = 64
TOKEN_BLOCK = 256
HALO = 8

ML_HEADS = 4
ML_HD = 256
ML_CHUNK = 64
ML_M_INIT = -1e30

LRU_BLOCKS = 16
LRU_BD = 64
LRU_CONV = 4
LRU_C = 8.0
LRU_GROUP = 256
RW_HD = 64
RW_HEADS = 16
RW_PAIRS = 8
RW_CHUNK = 128
RW_RANK = 128
RW_DECAY_SCALE = math.exp(-0.5)
RW_LN_EPS = 64e-5

VMEM_LIMIT = 56 * 1024 * 1024


def _cparams(*sem):
    return pltpu.CompilerParams(dimension_semantics=sem, vmem_limit_bytes=VMEM_LIMIT)


def _dot(a, b):
    return jnp.dot(a, b, preferred_element_type=F32)


def _dot_nt(a, b):
    return lax.dot_general(a, b, (((1,), (1,)), ((), ())), preferred_element_type=F32)


def _split2(x):
    h = x.astype(BF16)
    return h, (x - h.astype(F32)).astype(BF16)


def _tri_sum_left(tri, x):
    w = x.shape[1]
    y = _dot(tri, jnp.concatenate(_split2(x), axis=1))
    return y[:, :w] + y[:, w:]


def _tri_sum_right(x, tri):
    r = x.shape[0]
    y = _dot(jnp.concatenate(_split2(x), axis=0), tri)
    return y[:r] + y[r:]


def _group_sum(x, ones_bd):
    n = x.shape[0]
    h = x.astype(BF16)
    l = (x - h.astype(F32)).astype(BF16)
    y = _dot(jnp.concatenate([h, l], axis=0), ones_bd)
    return y[:n] + y[n:]


def _head_stack(z, head0):
    zero = jnp.zeros_like(z)
    return jnp.concatenate([jnp.where(head0, z, zero), jnp.where(head0, zero, z)], axis=0)


def _bwd_block(j, nc, nb):
    return jnp.where(j < nc, nc - 1 - j, nb + nc - 1 - j)


def _block_fn(reverse, nc, nb):
    if reverse:
        return lambda j: _bwd_block(j, nc, nb)
    return lambda j: j


def _mod_kernel(c_ref, w_ref, b_ref, o_ref):
    c = c_ref[...]
    s = (c * jax.nn.sigmoid(c)).astype(BF16)
    o_ref[0] = _dot(s, w_ref[0].astype(BF16)) + b_ref[0]


def _adaln_all(cond, w_mod, b_mod):
    L, D, D6 = w_mod.shape
    BP = cond.shape[0]
    tn = 1536
    return pl.pallas_call(
        _mod_kernel,
        grid=(L, D6 // tn),
        in_specs=[
            pl.BlockSpec((BP, D), lambda l, n: (0, 0)),
            pl.BlockSpec((1, D, tn), lambda l, n: (l, 0, n)),
            pl.BlockSpec((1, 1, tn), lambda l, n: (l, 0, n)),
        ],
        out_specs=pl.BlockSpec((1, BP, tn), lambda l, n: (l, 0, n)),
        out_shape=jax.ShapeDtypeStruct((L, BP, D6), F32),
        compiler_params=_cparams("arbitrary", "arbitrary"),
    )(cond, w_mod, b_mod.reshape(L, 1, D6))


def _rms(x, w):
    return x * lax.rsqrt(jnp.mean(x * x, axis=-1, keepdims=True) + NORM_EPS) * w


def _prenorm_kernel(x_ref, w_ref, mod_ref, h_ref):
    m = mod_ref[0]
    y = _rms(x_ref[0], w_ref[...])
    h_ref[0] = (y * (1.0 + m[1:2]) + m[0:1]).astype(BF16)


def _prenorm(x, norm_w, mod, nc):
    B, TA, D = x.shape
    nb = TA // TOKEN_BLOCK
    return pl.pallas_call(
        _prenorm_kernel,
        grid=(B, nb),
        in_specs=[
            pl.BlockSpec((1, TOKEN_BLOCK, D), lambda b, j: (b, j, 0)),
            pl.BlockSpec((1, D), lambda b, j: (0, 0)),
            pl.BlockSpec((1, 6, D), lambda b, j: (jnp.where(j < nc, B, b), 0, 0)),
        ],
        out_specs=pl.BlockSpec((1, TOKEN_BLOCK, D), lambda b, j: (b, j, 0)),
        out_shape=jax.ShapeDtypeStruct((B, TA, D), BF16),
        compiler_params=_cparams("arbitrary", "arbitrary"),
    )(x, norm_w.reshape(1, D), mod)


def _mm_kernel(h_ref, w_ref, o_ref):
    o_ref[...] = _dot(h_ref[...], w_ref[...]).astype(o_ref.dtype)


def _pick(n, cands):
    for c in cands:
        if n % c == 0:
            return c
    raise ValueError(f"no block size for {n}")


def _matmul(h, w, out_dtype, tn):
    N, K = h.shape
    M = w.shape[1]
    tm = _pick(N, (2048, 1024, 512, 256))
    return pl.pallas_call(
        _mm_kernel,
        grid=(M // tn, N // tm),
        in_specs=[
            pl.BlockSpec((tm, K), lambda n, m: (m, 0)),
            pl.BlockSpec((K, tn), lambda n, m: (0, n)),
        ],
        out_specs=pl.BlockSpec((tm, tn), lambda n, m: (m, n)),
        out_shape=jax.ShapeDtypeStruct((N, M), out_dtype),
        compiler_params=_cparams("arbitrary", "arbitrary"),
    )(h, w)


def _mlstm_kernel(reverse, *refs):
    if reverse:
        (q_ref, k_ref, v_ref, kt_ref, gc_ref, gr_ref, bc_ref, br_ref, yf_ref, o_ref, nw_ref,
         out_ref, ct_ref, n_ref, m_ref) = refs
    else:
        (q_ref, k_ref, v_ref, kt_ref, gc_ref, gr_ref, bc_ref, br_ref,
         out_ref, ct_ref, n_ref, m_ref) = refs
    C = ML_CHUNK
    j = pl.program_id(1)

    @pl.when(j == 0)
    def _():
        ct_ref[...] = jnp.zeros_like(ct_ref)
        n_ref[...] = jnp.zeros_like(n_ref)
        m_ref[...] = jnp.full_like(m_ref, ML_M_INIT)

    row = lax.broadcasted_iota(jnp.int32, (C, C), 0)
    col = lax.broadcasted_iota(jnp.int32, (C, C), 1)
    causal = (col >= row) if reverse else (col <= row)
    tri_l = causal.astype(BF16)
    tri_r = ((row >= col) if reverse else (row <= col)).astype(BF16)
    last = 0 if reverse else C - 1
    d = 1 if reverse else 0

    n_chunks = TOKEN_BLOCK // C
    order = range(n_chunks - 1, -1, -1) if reverse else range(n_chunks)
    heads = range(ML_HEADS)
    items = [(c, h) for c in order for h in heads]
    lanes = lambda h: slice(h * ML_HD, (h + 1) * ML_HD)
    rows = lambda c: slice(c * C, (c + 1) * C)

    g = {}
    for c in order:
        gc = gc_ref[0, rows(c), :] + bc_ref[...]
        gr = gr_ref[0, c] + br_ref[...]
        fc = _tri_sum_left(tri_l, jax.nn.log_sigmoid(gc))
        fr = _tri_sum_right(jax.nn.log_sigmoid(gr), tri_r)
        for h in heads:
            gi = d * ML_HEADS + h
            g[(c, h)] = dict(i_c=gc[:, gi:gi + 1], f_c=fc[:, 8 + gi:9 + gi],
                             i_r=gr[gi:gi + 1, :], f_r=fr[8 + gi:9 + gi, :])
    for it in items:
        e = g[it]
        e["f_end"] = e["f_c"][last:last + 1, :]
        logw = jnp.where(causal, e["f_c"] - e["f_r"] + e["i_r"], -jnp.inf)
        e["m_loc"] = jnp.max(logw, axis=1, keepdims=True)
        e["w_loc"] = jnp.exp(logw - e["m_loc"])
        log_end_r = e["f_end"] - e["f_r"] + e["i_r"]
        e["m_end"] = jnp.max(log_end_r, axis=1, keepdims=True)
        e["w_end"] = jnp.exp(e["f_end"] - e["f_c"] + e["i_c"] - e["m_end"])
    for c, h in items:
        e = g[(c, h)]
        e["q"] = q_ref[0, rows(c), lanes(h)]
        e["s"] = _dot_nt(e["q"], k_ref[0, rows(c), lanes(h)]) * e["w_loc"]
    for c, h in items:
        e = g[(c, h)]
        e["num_loc"] = _dot(e["s"].astype(BF16), v_ref[0, rows(c), lanes(h)])
        e["den_loc"] = jnp.sum(e["s"], axis=1, keepdims=True)
    for c, h in items:
        e = g[(c, h)]
        wv = (e["w_end"] * v_ref[0, rows(c), lanes(h)].astype(F32)).astype(BF16)
        pad = [jnp.zeros((C, ML_HD), BF16)] * n_chunks
        pad[c] = wv
        e["kv_loc"] = _dot(kt_ref[0, lanes(h), :], jnp.concatenate(pad, axis=0))
        e["nk_loc"] = jnp.sum(e["w_end"] * k_ref[0, rows(c), lanes(h)].astype(F32), axis=0, keepdims=True)

    ct = {h: ct_ref[h] for h in heads}
    nv = {h: n_ref[h][0:1, :] for h in heads}
    ms = {h: m_ref[h][0:1, 0:1] for h in heads}
    h_out = {}
    for c in order:
        qc = {h: _dot(g[(c, h)]["q"], ct[h].astype(BF16)) for h in heads}
        for h in heads:
            e = g[(c, h)]
            inter = e["f_c"] + ms[h]
            m_t = jnp.maximum(e["m_loc"], inter)
            a_loc = jnp.exp(e["m_loc"] - m_t)
            w_inter = jnp.exp(inter - m_t)
            qn = jnp.sum(e["q"].astype(F32) * nv[h], axis=1, keepdims=True)
            num = a_loc * e["num_loc"] + w_inter * qc[h]
            den = a_loc * e["den_loc"] + w_inter * qn
            h_out[(c, h)] = num / jnp.maximum(jnp.abs(den), jnp.exp(-m_t))
        for h in heads:
            e = g[(c, h)]
            m_new = jnp.maximum(e["f_end"] + ms[h], e["m_end"])
            scale = jnp.exp(e["m_end"] - m_new)
            decay = jnp.exp(e["f_end"] + ms[h] - m_new)
            ct[h] = decay * ct[h] + scale * e["kv_loc"]
            nv[h] = decay * nv[h] + scale * e["nk_loc"]
            ms[h] = m_new
    for h in heads:
        ct_ref[h] = ct[h]
        n_ref[h] = jnp.broadcast_to(nv[h], n_ref.shape[1:])
        m_ref[h] = jnp.broadcast_to(ms[h], m_ref.shape[1:])

    for c, h in items:
        if reverse:
            tot = yf_ref[0, rows(c), lanes(h)] + h_out[(c, h)]
            hn = tot * lax.rsqrt(jnp.mean(tot * tot, axis=-1, keepdims=True) + NORM_EPS) * nw_ref[:, lanes(h)]
            out_ref[0, rows(c), lanes(h)] = (hn * jax.nn.sigmoid(o_ref[0, rows(c), lanes(h)])).astype(out_ref.dtype)
        else:
            out_ref[0, rows(c), lanes(h)] = h_out[(c, h)]


def _mlstm(qkv, gcol, grow, bcol, brow, o, norm_w, nc):
    B, TA, W3 = qkv.shape
    W = W3 // 3
    nb = TA // TOKEN_BLOCK
    cpb = TOKEN_BLOCK // ML_CHUNK
    scratch = [
        pltpu.VMEM((ML_HEADS, ML_HD, ML_HD), F32),
        pltpu.VMEM((ML_HEADS, 8, ML_HD), F32),
        pltpu.VMEM((ML_HEADS, 8, 128), F32),
    ]
    kt = jnp.swapaxes(qkv[:, :, W:2 * W], 1, 2)

    def run(reverse, extra_in, extra_specs, out_dtype):
        blk = _block_fn(reverse, nc, nb)
        tok = lambda w, cb: pl.BlockSpec((1, TOKEN_BLOCK, w), lambda b, j: (b, blk(j), cb))
        in_specs = [
            tok(W, 0), tok(W, 1), tok(W, 2),
            pl.BlockSpec((1, W, TOKEN_BLOCK), lambda b, j: (b, 0, blk(j))),
            pl.BlockSpec((1, TOKEN_BLOCK, 128), lambda b, j: (b, blk(j), 0)),
            pl.BlockSpec((1, cpb, 16, ML_CHUNK), lambda b, j: (b, blk(j), 0, 0)),
            pl.BlockSpec((1, 128), lambda b, j: (0, 0)),
            pl.BlockSpec((16, 1), lambda b, j: (0, 0)),
        ] + extra_specs(tok)
        return pl.pallas_call(
            functools.partial(_mlstm_kernel, reverse),
            grid=(B, nb),
            in_specs=in_specs,
            out_specs=tok(W, 0),
            out_shape=jax.ShapeDtypeStruct((B, TA, W), out_dtype),
            scratch_shapes=scratch,
            compiler_params=_cparams("arbitrary", "arbitrary"),
        )(qkv, qkv, qkv, kt, gcol, grow, bcol, brow, *extra_in)

    yf = run(False, (), lambda tok: [], F32)
    return run(True, (yf, o, norm_w.reshape(1, W)),
               lambda tok: [tok(W, 0), tok(W, 0), pl.BlockSpec((1, W), lambda b, j: (0, 0))], BF16)


def _halo_specs(width, blk, TA):
    per = TOKEN_BLOCK // HALO
    last = TA // HALO - 1
    prev = pl.BlockSpec((1, HALO, width), lambda b, j: (b, jnp.maximum(blk(j) * per - 1, 0), 0))
    nxt = pl.BlockSpec((1, HALO, width), lambda b, j: (b, jnp.minimum((blk(j) + 1) * per, last), 0))
    return prev, nxt


def _lru_kernel(reverse, nc, nb, *refs):
    if reverse:
        (x_ref, xp_ref, xn_ref, cw_ref, cb_ref, wr_ref, wi_ref, br_ref, bi_ref, lam_ref,
         hf_ref, gate_ref, out_ref, xpad_s, a_s, b_s, y_s, h_s) = refs
    else:
        (x_ref, xp_ref, xn_ref, cw_ref, cb_ref, wr_ref, wi_ref, br_ref, bi_ref, lam_ref,
         out_ref, xpad_s, a_s, b_s, y_s, h_s) = refs
    TB = TOKEN_BLOCK
    W = x_ref.shape[-1]
    n_groups = W // LRU_GROUP
    j = pl.program_id(1)
    wr = lax.rem(j, 2)
    rd = 1 - wr
    jin = jnp.minimum(j, nb - 1)
    blk = _bwd_block(jin, nc, nb) if reverse else jin

    @pl.when(j == 0)
    def _():
        h_s[...] = jnp.zeros_like(h_s)
        a_s[1] = jnp.zeros(a_s.shape[1:], F32)
        b_s[1] = jnp.zeros(b_s.shape[1:], F32)

    seg_start = (blk == 0) | (blk == nc)
    seg_end = (blk == nc - 1) | (blk == nb - 1)
    xpad_s[0:HALO, :] = jnp.where(seg_start, 0.0, xp_ref[0])
    xpad_s[HALO:HALO + TB, :] = x_ref[0]
    xpad_s[HALO + TB:, :] = jnp.where(seg_end, 0.0, xn_ref[0])
    left = LRU_CONV // 2

    def prepare_group(g):
        cols = slice(g * LRU_GROUP, (g + 1) * LRU_GROUP)
        u = cb_ref[:, cols]
        for tap in range(LRU_CONV):
            off = HALO - left + tap
            u = u + cw_ref[tap:tap + 1, cols] * xpad_s[off:off + TB, cols]
        ug = u.astype(BF16)
        r = jax.nn.sigmoid(_dot(ug, wr_ref[g]) + br_ref[:, cols])
        i = jax.nn.sigmoid(_dot(ug, wi_ref[g]) + bi_ref[:, cols])
        a = jnp.exp(-LRU_C * r * jax.nn.softplus(-lam_ref[:, cols]))
        a_s[wr, :, cols] = a
        b_s[wr, :, cols] = jnp.sqrt(jnp.maximum(1.0 - a * a, 0.0)) * (i * u)

    def scan_rows(lo, hi, h):
        for t in range(lo, hi):
            idx = (TB - 1 - t) if reverse else t
            h = a_s[rd, idx:idx + 1, :] * h + b_s[rd, idx:idx + 1, :]
            y_s[idx:idx + 1, :] = h
        return h

    h = h_s[...]
    per = TB // n_groups
    for g in range(n_groups):
        h = scan_rows(g * per, (g + 1) * per, h)
        prepare_group(g)
    h_s[...] = h
    if reverse:
        out_ref[0] = ((hf_ref[0] + y_s[...]) * jax.nn.gelu(gate_ref[0])).astype(out_ref.dtype)
    else:
        out_ref[0] = y_s[...]


def _lru(x, gate, cw, cb, wr, wi, br, bi, lam, nc):
    B, TA, W = x.shape
    nb = TA // TOKEN_BLOCK
    ng = W // LRU_GROUP
    scratch = [
        pltpu.VMEM((TOKEN_BLOCK + 2 * HALO, W), F32),
        pltpu.VMEM((2, TOKEN_BLOCK, W), F32),
        pltpu.VMEM((2, TOKEN_BLOCK, W), F32),
        pltpu.VMEM((TOKEN_BLOCK, W), F32),
        pltpu.VMEM((1, W), F32),
    ]

    def run(d, extra_in, out_dtype):
        reverse = d == 1
        blk = _block_fn(reverse, nc, nb)
        blk_in = lambda j: blk(jnp.minimum(j, nb - 1))
        blk_out = lambda j: blk(jnp.maximum(j - 1, 0))
        tok = pl.BlockSpec((1, TOKEN_BLOCK, W), lambda b, j: (b, blk_in(j), 0))
        otok = pl.BlockSpec((1, TOKEN_BLOCK, W), lambda b, j: (b, blk_out(j), 0))
        prev, nxt = _halo_specs(W, blk_in, TA)
        const = lambda shape: pl.BlockSpec(shape, lambda b, j: (0,) * len(shape))
        in_specs = [tok, prev, nxt, const((LRU_CONV, W)), const((1, W)),
                    const((ng, LRU_GROUP, LRU_GROUP)), const((ng, LRU_GROUP, LRU_GROUP)),
                    const((1, W)), const((1, W)), const((1, W))] + [otok] * len(extra_in)
        return pl.pallas_call(
            functools.partial(_lru_kernel, reverse, nc, nb),
            grid=(B, nb + 1),
            in_specs=in_specs,
            out_specs=otok,
            out_shape=jax.ShapeDtypeStruct((B, TA, W), out_dtype),
            scratch_shapes=scratch,
            compiler_params=_cparams("arbitrary", "arbitrary"),
        )(x, x, x, cw, cb.reshape(1, W), wr[d], wi[d], br[d:d + 1], bi[d:d + 1], lam[d:d + 1], *extra_in)

    hf = run(0, (), F32)
    return run(1, (hf, gate), BF16)


def _rwkv_kernel(reverse, nc, nb, *refs):
    if reverse:
        (x_ref, xp_ref, xn_ref, mu_ref, kk_ref, ka_ref, d0_ref, du_ref, a0_ref, au_ref,
         a0f_ref, auf_ref, gu_ref, rk_ref, lnw_ref, lnb_ref, yf_ref,
         out_ref, xpad_s, xs_s, ys_s, op_s, bk_s, dec_s, v_s, bon_s, g_s, st_s) = refs
        handoff = (xs_s, ys_s, op_s, bk_s, dec_s, v_s, bon_s, g_s)
    else:
        (x_ref, xp_ref, xn_ref, mu_ref, kk_ref, ka_ref, d0_ref, du_ref, a0_ref, au_ref,
         out_ref, xpad_s, xs_s, ys_s, op_s, bk_s, dec_s, st_s) = refs
        handoff = (xs_s, ys_s, op_s, bk_s, dec_s)
    TB = TOKEN_BLOCK
    C = RW_CHUNK
    P = 2 * RW_HD
    D = RW_HEADS * RW_HD
    n_chunks = TB // C
    j = pl.program_id(1)
    wr = lax.rem(j, 2)
    rd = 1 - wr
    jin = jnp.minimum(j, nb - 1)
    blk = _bwd_block(jin, nc, nb) if reverse else jin

    @pl.when(j == 0)
    def _():
        st_s[...] = jnp.zeros_like(st_s)
        for ref in handoff:
            ref[1] = jnp.zeros(ref.shape[1:], ref.dtype)

    lane = lax.broadcasted_iota(jnp.int32, (P, P), 1)
    sub = lax.broadcasted_iota(jnp.int32, (P, P), 0)
    same_head = (lane < RW_HD) == (sub < RW_HD)
    ones_bd = same_head.astype(BF16)
    row = lax.broadcasted_iota(jnp.int32, (C, C), 0)
    col = lax.broadcasted_iota(jnp.int32, (C, C), 1)
    incl = (col >= row) if reverse else (col <= row)
    strict = (col > row) if reverse else (col < row)
    tri = incl.astype(BF16)
    eye = (row == col).astype(F32)
    merge_masks = []
    for i in range(int(math.log2(C))):
        same_parent = jnp.right_shift(row, i + 1) == jnp.right_shift(col, i + 1)
        merge_masks.append(same_parent & (jnp.right_shift(row, i) != jnp.right_shift(col, i)))
    lane_lo = lax.broadcasted_iota(jnp.int32, (1, P), 1) < RW_HD
    head_masks = (lane_lo, jnp.logical_not(lane_lo))
    lane2 = lax.broadcasted_iota(jnp.int32, (1, 2 * P), 1)
    lane_lo2 = jnp.bitwise_and(lane2, P - 1) < RW_HD
    mid = C // 2
    order = range(n_chunks - 1, -1, -1) if reverse else range(n_chunks)
    items = [(p, c) for p in range(RW_PAIRS) for c in range(n_chunks)]
    slot_of = lambda p, c: p * n_chunks + c

    pre = {}

    def consume_gram():
        for p, c in items:
            i = slot_of(p, c)
            ys = ys_s[rd, i]
            d = dict(kap_st=op_s[rd, 3 * i], r_st=op_s[rd, 3 * i + 1], vb=op_s[rd, 3 * i + 2],
                     bk=bk_s[rd, i], dec=dec_s[rd, i], heads=[])
            for h in range(2):
                g = _dot_nt(xs_s[rd, 2 * i + h], ys)
                a = jnp.where(strict, g[:C, :C], 0.0)
                d["heads"].append(dict(
                    ab=a.astype(BF16),
                    t=eye - jnp.where(merge_masks[0], a, 0.0),
                    bm_rk=jnp.concatenate([jnp.where(strict, g[:C, C:], 0.0),
                                           jnp.where(incl, g[C:, C:], 0.0)], axis=0).astype(BF16),
                    rb=jnp.where(incl, g[C:, :C], 0.0).astype(BF16),
                ))
            pre[(p, c)] = d

    zero_b = jnp.zeros((C, C), BF16)

    def inverse_level_a(mm):
        for it in items:
            for ch in pre[it]["heads"]:
                ch["tb"] = ch["t"].astype(BF16)
                ch["m1"] = _dot(jnp.where(mm, ch["ab"], zero_b), ch["tb"]).astype(BF16)

    def inverse_level_b():
        for it in items:
            for ch in pre[it]["heads"]:
                ch["t"] = ch["t"] - _dot(ch["tb"], ch["m1"])

    def consume_apply():
        for it in items:
            d = pre[it]
            h0, h1 = d["heads"]
            d["x"] = _dot(jnp.concatenate([h0["bm_rk"], h1["bm_rk"]], axis=1), _head_stack(d["vb"], lane_lo))
        for it in items:
            d = pre[it]
            h0, h1 = d["heads"]
            t2 = jnp.concatenate([h0["t"], h1["t"]], axis=1).astype(BF16)
            rhs = jnp.concatenate([d["kap_st"], d["x"][:C].astype(BF16)], axis=1)
            d["tk"] = _dot(t2, _head_stack(rhs, lane_lo2))
        for it in items:
            d = pre[it]
            h0, h1 = d["heads"]
            d["w_st"] = d["tk"][:, :P].astype(BF16)
            d["rb2"] = jnp.concatenate([h0["rb"], h1["rb"]], axis=1)

    def consume_recurrence():
        inv = 1.0 / RW_HD
        pairs = range(RW_PAIRS)
        st = {p: st_s[p] for p in pairs}
        ys_out = {}
        for c in order:
            ws_rs, ub, upd = {}, {}, {}
            for p in pairs:
                d = pre[(p, c)]
                ws_rs[p] = _dot(jnp.concatenate([d["w_st"], d["r_st"]], axis=0), st[p].astype(BF16))
            for p in pairs:
                ub[p] = (-(ws_rs[p][:C] + pre[(p, c)]["tk"][:, P:])).astype(BF16)
            for p in pairs:
                d = pre[(p, c)]
                upd[p] = _dot(d["bk"], jnp.concatenate([ub[p], d["vb"]], axis=0))
            for p in pairs:
                st[p] = st[p] * pre[(p, c)]["dec"] + jnp.where(same_head, upd[p], 0.0)
            for p in pairs:
                d = pre[(p, c)]
                ys_out[(p, c)] = ws_rs[p][C:] + d["x"][C:] + _dot(d["rb2"], _head_stack(ub[p], lane_lo))
        for p in pairs:
            st_s[p] = st[p]
        if reverse:
            y = {p: yf_ref[0, p] + jnp.concatenate([ys_out[(p, c)] for c in range(n_chunks)], axis=0) for p in pairs}
            group_sum1 = lambda z: _dot(z.astype(BF16), ones_bd)
            dlt = {p: y[p] - group_sum1(y[p]) * inv for p in pairs}
            var = {p: group_sum1(dlt[p] * dlt[p]) * inv for p in pairs}
            bonus = {p: group_sum1(bon_s[rd, p]) * v_s[rd, p] for p in pairs}
            for p in pairs:
                sl = slice(p * P, (p + 1) * P)
                yn = dlt[p] * lax.rsqrt(var[p] + RW_LN_EPS) * lnw_ref[:, sl] + lnb_ref[:, sl]
                out_ref[0, :, sl] = ((yn + bonus[p]) * g_s[rd, p]).astype(out_ref.dtype)
        else:
            for p in pairs:
                for c in range(n_chunks):
                    out_ref[0, p, c * C:(c + 1) * C, :] = ys_out[(p, c)]

    def shifted(cols):
        m = mu_ref[:, cols]
        nbr = xpad_s[HALO - 1:HALO - 1 + TB, cols] + xpad_s[HALO + 1:HALO + 1 + TB, cols]
        return xpad_s[HALO:HALO + TB, cols] * (1.0 - m) + nbr * (0.5 * m)

    low = {}

    def prepare_common():
        seg_start = (blk == 0) | (blk == nc)
        seg_end = (blk == nc - 1) | (blk == nb - 1)
        xpad_s[0:HALO, :] = jnp.where(seg_start, 0.0, xp_ref[0])
        xpad_s[HALO:HALO + TB, :] = x_ref[0]
        xpad_s[HALO + TB:, :] = jnp.where(seg_end, 0.0, xn_ref[0])
        low["dd"] = jnp.tanh(shifted(slice(3 * D, 3 * D + RW_RANK))).astype(BF16)
        low["ad"] = shifted(slice(3 * D + RW_RANK, 3 * D + 2 * RW_RANK)).astype(BF16)
        if reverse:
            low["gd"] = jax.nn.sigmoid(shifted(slice(3 * D + 2 * RW_RANK, 3 * D + 3 * RW_RANK))).astype(BF16)

    def prepare_pair(p):
        sl = slice(p * P, (p + 1) * P)
        r = shifted(slice(p * P, (p + 1) * P))
        k = shifted(slice(D + p * P, D + (p + 1) * P))
        v = shifted(slice(2 * D + p * P, 2 * D + (p + 1) * P))
        logw = -RW_DECAY_SCALE * jax.nn.sigmoid(d0_ref[:, sl] + _dot(low["dd"], du_ref[:, sl]))
        a = jax.nn.sigmoid(a0_ref[:, sl] + _dot(low["ad"], au_ref[:, sl]))
        kt = k * (1.0 + (a - 1.0) * ka_ref[:, sl])
        if reverse:
            a_f = jax.nn.sigmoid(a0f_ref[:, sl] + _dot(low["ad"], auf_ref[:, sl]))
            bon_s[wr, p] = r * (kt + k * (1.0 + (a_f - 1.0) * ka_ref[:, sl])) * rk_ref[:, sl]
            g_s[wr, p] = _dot(low["gd"], gu_ref[:, sl])
            v_s[wr, p] = v
        kkp = k * kk_ref[:, sl]
        kap_p = kkp / jnp.maximum(jnp.sqrt(_group_sum(kkp * kkp, ones_bd)), 1e-12)
        b_p = kap_p * a
        for c in range(n_chunks):
            i = slot_of(p, c)
            rows = slice(c * C, (c + 1) * C)
            lw = logw[rows]
            rr = r[rows]
            kap = kap_p[rows]
            ktc = kt[rows]
            bb = b_p[rows]
            L = _tri_sum_left(tri, lw)
            Lx = L - lw
            Lm = L[mid:mid + 1, :]
            Lc = jnp.sum(lw, axis=0, keepdims=True)
            e_neg = jnp.exp(Lm - L)
            e_end = jnp.exp(Lc - L)
            xs = jnp.concatenate([kap * jnp.exp(Lx - Lm), rr * jnp.exp(L - Lm)], axis=0)
            for h, hm in enumerate(head_masks):
                xs_s[wr, 2 * i + h] = jnp.where(hm, xs, 0.0).astype(BF16)
            ys_s[wr, i] = jnp.concatenate([bb * e_neg, ktc * e_neg], axis=0).astype(BF16)
            op_s[wr, 3 * i] = (kap * jnp.exp(Lx)).astype(BF16)
            op_s[wr, 3 * i + 1] = (rr * jnp.exp(L)).astype(BF16)
            op_s[wr, 3 * i + 2] = v[rows].astype(BF16)
            bk_s[wr, i] = jnp.concatenate([(bb * e_end).T, (ktc * e_end).T], axis=1).astype(BF16)
            dec_s[wr, i] = jnp.exp(jnp.broadcast_to(Lc, (P, P)).T)

    prepare_common()
    consume_gram()
    todo = list(range(RW_PAIRS))
    for mm in merge_masks[1:]:
        inverse_level_a(mm)
        if todo:
            prepare_pair(todo.pop(0))
        inverse_level_b()
    consume_apply()
    if todo:
        prepare_pair(todo.pop(0))
    consume_recurrence()
    for p in todo:
        prepare_pair(p)


def _rwkv(x, mu, k_k, k_a, d0, du, a0, au, gate_up, r_k, ln_w, ln_b, nc):
    B, TA, WS = x.shape
    D = RW_HEADS * RW_HD
    P = 2 * RW_HD
    C = RW_CHUNK
    nb = TA // TOKEN_BLOCK
    n_items = RW_PAIRS * (TOKEN_BLOCK // C)
    const = lambda shape: pl.BlockSpec(shape, lambda b, j: (0,) * len(shape))
    row = lambda a: a.reshape(1, -1)
    slab = lambda: pltpu.VMEM((2, RW_PAIRS, TOKEN_BLOCK, P), F32)

    def run(d, extra_in, extra_specs, out_spec, out_shape, extra_scratch):
        reverse = d == 1
        blk = _block_fn(reverse, nc, nb)
        blk_in = lambda j: blk(jnp.minimum(j, nb - 1))
        blk_out = lambda j: blk(jnp.maximum(j - 1, 0))
        tok = pl.BlockSpec((1, TOKEN_BLOCK, WS), lambda b, j: (b, blk_in(j), 0))
        prev, nxt = _halo_specs(WS, blk_in, TA)
        in_specs = [tok, prev, nxt, const((1, WS)), const((1, D)), const((1, D)),
                    const((1, D)), const((RW_RANK, D)), const((1, D)), const((RW_RANK, D))] + extra_specs(blk_out)
        scratch = [
            pltpu.VMEM((TOKEN_BLOCK + 2 * HALO, WS), F32),
            pltpu.VMEM((2, 2 * n_items, 2 * C, P), BF16),
            pltpu.VMEM((2, n_items, 2 * C, P), BF16),
            pltpu.VMEM((2, 3 * n_items, C, P), BF16),
            pltpu.VMEM((2, n_items, P, 2 * C), BF16),
            pltpu.VMEM((2, n_items, P, P), F32),
        ] + extra_scratch + [pltpu.VMEM((RW_PAIRS, P, P), F32)]
        return pl.pallas_call(
            functools.partial(_rwkv_kernel, reverse, nc, nb),
            grid=(B, nb + 1),
            in_specs=in_specs,
            out_specs=out_spec(blk_out),
            out_shape=out_shape,
            scratch_shapes=scratch,
            compiler_params=_cparams("arbitrary", "arbitrary"),
        )(x, x, x, row(mu), row(k_k), row(k_a), d0[d:d + 1], du[d], a0[d:d + 1], au[d], *extra_in)

    pair_spec = lambda blk: pl.BlockSpec((1, RW_PAIRS, TOKEN_BLOCK, P), lambda b, j: (b, 0, blk(j), 0))
    yf = run(0, (), lambda blk: [], pair_spec, jax.ShapeDtypeStruct((B, RW_PAIRS, TA, P), F32), [])
    return run(
        1,
        (a0[0:1], au[0], gate_up, row(r_k), row(ln_w), row(ln_b), yf),
        lambda blk: [const((1, D)), const((RW_RANK, D)), const((RW_RANK, D)), const((1, D)), const((1, D)),
                     const((1, D)), pair_spec(blk)],
        lambda blk: pl.BlockSpec((1, TOKEN_BLOCK, D), lambda b, j: (b, blk(j), 0)),
        jax.ShapeDtypeStruct((B, TA, D), BF16),
        [slab(), slab(), slab()],
    )


def _merge_kernel(x_ref, yml_ref, ylru_ref, yrw_ref, g_ref, oml_ref, olru_ref, orw_ref, wout_ref,
                  mod_ref, nw_ref, xo_ref, h_ref):
    D = x_ref.shape[-1]
    g = g_ref[0]
    acc = jax.nn.sigmoid(g[:, 0:D]) * _dot(yml_ref[0], oml_ref[...])
    acc = acc + jax.nn.sigmoid(g[:, D:2 * D]) * _dot(ylru_ref[0], olru_ref[...])
    acc = acc + jax.nn.sigmoid(g[:, 2 * D:3 * D]) * _dot(yrw_ref[0], orw_ref[...])
    y = _dot(acc.astype(BF16), wout_ref[...])
    m = mod_ref[0]
    xn = x_ref[0] + m[2:3] * y
    xo_ref[0] = xn
    h_ref[0] = (_rms(xn, nw_ref[...]) * (1.0 + m[4:5]) + m[3:4]).astype(BF16)


def _merge(x, yml, ylru, yrw, gates, oml, olru, orw, wout, mod, norm2_w, nc, skip_ctx):
    B, TA, D = x.shape
    nb = TA // TOKEN_BLOCK
    off = nc if skip_ctx else 0
    rows_out = (nb - off) * TOKEN_BLOCK
    tok = lambda w: pl.BlockSpec((1, TOKEN_BLOCK, w), lambda b, j: (b, j + off, 0))
    otok = pl.BlockSpec((1, TOKEN_BLOCK, D), lambda b, j: (b, j, 0))
    wspec = pl.BlockSpec((D, D), lambda b, j: (0, 0))
    return pl.pallas_call(
        _merge_kernel,
        grid=(B, nb - off),
        in_specs=[tok(D), tok(D), tok(D), tok(D), tok(3 * D), wspec, wspec, wspec, wspec,
                  pl.BlockSpec((1, 6, D), lambda b, j: (jnp.where(j + off < nc, B, b), 0, 0)),
                  pl.BlockSpec((1, D), lambda b, j: (0, 0))],
        out_specs=[otok, otok],
        out_shape=[jax.ShapeDtypeStruct((B, rows_out, D), F32), jax.ShapeDtypeStruct((B, rows_out, D), BF16)],
        input_output_aliases={} if skip_ctx else {0: 0},
        compiler_params=_cparams("arbitrary", "arbitrary"),
    )(x, yml, ylru, yrw, gates, oml, olru, orw, wout, mod, norm2_w.reshape(1, D))


def _ffn_kernel(final, x_ref, h_ref, win_ref, wout_ref, mod_ref, *rest):
    xo_ref = rest[-1]
    F = wout_ref.shape[0]
    u = _dot(h_ref[0], win_ref[...])
    gate = u[:, :F]
    act = (gate * jax.nn.sigmoid(gate) * u[:, F:]).astype(BF16)
    xn = x_ref[0] + mod_ref[0][5:6] * _dot(act, wout_ref[...])
    xo_ref[0] = _rms(xn, rest[0][...]) if final else xn


def _ffn(x, h, win, wout, mod, nc, final_w=None):
    B, TA, D = x.shape
    F = wout.shape[0]
    nb = TA // TOKEN_BLOCK
    final = final_w is not None
    tok = pl.BlockSpec((1, TOKEN_BLOCK, D), lambda b, j: (b, j, 0))
    extra_specs = [pl.BlockSpec((1, D), lambda b, j: (0, 0))] if final else []
    extra_in = (final_w.reshape(1, D),) if final else ()
    return pl.pallas_call(
        functools.partial(_ffn_kernel, final),
        grid=(B, nb),
        in_specs=[tok, tok,
                  pl.BlockSpec((D, 2 * F), lambda b, j: (0, 0)),
                  pl.BlockSpec((F, D), lambda b, j: (0, 0)),
                  pl.BlockSpec((1, 6, D), lambda b, j: (jnp.where(j < nc, B, b), 0, 0))] + extra_specs,
        out_specs=tok,
        out_shape=jax.ShapeDtypeStruct((B, TA, D), F32),
        input_output_aliases={0: 0},
        compiler_params=_cparams("arbitrary", "arbitrary"),
    )(x, h, win, wout, mod, *extra_in)


def _grid_transpose_latent(xa, ctx_len, rows, cols):
    B, TA, D = xa.shape
    lat = xa[:, ctx_len:].reshape(B, rows, cols, D).transpose(0, 2, 1, 3).reshape(B, rows * cols, D)
    return jnp.concatenate([xa[:, :ctx_len], lat], axis=1)


def _block_diag_groups(w):
    per = LRU_GROUP // LRU_BD
    d2 = w.shape[0]
    wg = w.reshape(d2, LRU_BLOCKS // per, per, LRU_BD, LRU_BD)
    eye = jnp.eye(per, dtype=w.dtype)
    out = jnp.einsum('dgpio,pq->dgpiqo', wg, eye)
    return out.reshape(d2, LRU_BLOCKS // per, LRU_GROUP, LRU_GROUP).astype(BF16)


def _pad_dir_rows(w):
    z = jnp.zeros_like(w[0])
    return jnp.stack([jnp.concatenate([w[0], z], axis=0), jnp.concatenate([z, w[1]], axis=0)]).astype(BF16)


def kernel(x, c, ctx, c_ctx, w_mod, b_mod, norm1_w, norm2_w, w_in, ml_ig_b, ml_fg_b, ml_norm_w, lru_conv_w, lru_conv_b, lru_gr_w, lru_gr_b, lru_gi_w, lru_gi_b, lru_lambda, rw_mu, rw_decay0, rw_decay_up, rw_iclr0, rw_iclr_up, rw_gate_up, rw_k_k, rw_k_a, rw_r_k, rw_ln_w, rw_ln_b, out_ml, out_lru, out_rw, w_out, w_ffn_in, w_ffn_out, final_norm_w):
    B, T, D = x.shape
    CTX = ctx.shape[1]
    L = w_mod.shape[0]
    TA = CTX + T
    assert CTX % TOKEN_BLOCK == 0 and T % TOKEN_BLOCK == 0 and T % GRID_W == 0
    nc = CTX // TOKEN_BLOCK
    rows = T // GRID_W
    N = B * TA
    MLW = ML_HEADS * ML_HD

    BP = -(-(B + 1) // 8) * 8
    cond = jnp.concatenate([c, c_ctx[None, :], jnp.zeros((BP - B - 1, D), F32)], axis=0)
    mods = _adaln_all(cond, w_mod, b_mod).reshape(L, BP, 6, D)

    xa = jnp.concatenate([ctx, x], axis=1)
    column_order = False
    for layer in range(L):
        last = layer == L - 1
        want_columns = layer % 2 == 1
        if want_columns != column_order:
            xa = _grid_transpose_latent(xa, CTX, GRID_W if column_order else rows, rows if column_order else GRID_W)
            column_order = want_columns
        mod = mods[layer]

        wi = w_in[layer]
        o = 0
        wq = wi[:, o:o + MLW]; o += MLW
        wk = wi[:, o:o + MLW] * (ML_HD ** -0.5); o += MLW
        wv = wi[:, o:o + MLW]; o += MLW
        wo = wi[:, o:o + MLW]; o += MLW
        wif = wi[:, o:o + 4 * ML_HEADS]; o += 4 * ML_HEADS
        wlx = wi[:, o:o + D]; o += D
        wlg = wi[:, o:o + D]; o += D
        rw_w = 3 * D + 3 * RW_RANK
        wrw = wi[:, o:o + rw_w]; o += rw_w
        wmg = wi[:, o:o + 3 * D]; o += 3 * D
        w_qkv = jnp.concatenate([wq, wk, wv], axis=1).astype(BF16)
        w_if = jnp.concatenate([wif, jnp.zeros((D, 128 - 4 * ML_HEADS), F32)], axis=1).astype(BF16)

        h = _prenorm(xa, norm1_w[layer], mod, nc).reshape(N, D)
        qkv = _matmul(h, w_qkv, BF16, 1024).reshape(B, TA, 3 * MLW)
        ml_o = _matmul(h, wo.astype(BF16), F32, 1024).reshape(B, TA, MLW)
        gcol = _matmul(h, w_if, F32, 128).reshape(B, TA, 128)
        lru_x = _matmul(h, wlx.astype(BF16), F32, 1024).reshape(B, TA, D)
        lru_g = _matmul(h, wlg.astype(BF16), F32, 1024).reshape(B, TA, D)
        rw = _matmul(h, wrw.astype(BF16), F32, rw_w // 3).reshape(B, TA, rw_w)
        mgates = _matmul(h, wmg.astype(BF16), F32, 1024).reshape(B, TA, 3 * D)

        grow = gcol[:, :, :16].reshape(B, TA // ML_CHUNK, ML_CHUNK, 16).transpose(0, 1, 3, 2)
        gate_b = jnp.concatenate([ml_ig_b[layer].reshape(-1), ml_fg_b[layer].reshape(-1)])
        bcol = jnp.concatenate([gate_b, jnp.zeros((128 - 16,), F32)]).reshape(1, 128)
        y_ml = _mlstm(qkv, gcol, grow, bcol, gate_b.reshape(16, 1), ml_o, ml_norm_w[layer], nc)

        y_lru = _lru(lru_x, lru_g, lru_conv_w[layer], lru_conv_b[layer],
                     _block_diag_groups(lru_gr_w[layer]), _block_diag_groups(lru_gi_w[layer]),
                     lru_gr_b[layer], lru_gi_b[layer], lru_lambda[layer], nc)

        y_rw = _rwkv(rw, rw_mu[layer], rw_k_k[layer], rw_k_a[layer], rw_decay0[layer],
                     _pad_dir_rows(rw_decay_up[layer]), rw_iclr0[layer], _pad_dir_rows(rw_iclr_up[layer]),
                     rw_gate_up[layer].astype(BF16), rw_r_k[layer], rw_ln_w[layer], rw_ln_b[layer], nc)

        xa, h2 = _merge(xa, y_ml, y_lru, y_rw, mgates, out_ml[layer].astype(BF16), out_lru[layer].astype(BF16),
                        out_rw[layer].astype(BF16), w_out[layer].astype(BF16), mod, norm2_w[layer], nc, last)
        xa = _ffn(xa, h2, w_ffn_in[layer].astype(BF16), w_ffn_out[layer].astype(BF16), mod, 0 if last else nc,
                  final_norm_w if last else None)

    out = xa
    if column_order:
        out = out.reshape(B, GRID_W, rows, D).transpose(0, 2, 1, 3).reshape(B, T, D)
    return out
```
